```python
import jax, jax.numpy as jnp
from jax import lax
import numpy as np

D_MODEL = 1024
BATCH = 32
SEQ = 2048
DEPTH = 4

N_MIXERS = 2
N_HEADS = 8
QK_NOPE_DIM = 128
QK_ROPE_DIM = 64
V_HEAD_DIM = 128
Q_LORA_RANK = 512
KV_LORA_RANK = 256
ROPE_THETA = 10000.0
Q_BLOCK = 128
POOL_WINDOWS = (2, 4, 8, 16)
N_POOL_GROUPS = len(POOL_WINDOWS)
POOL_GROUP_DIM = D_MODEL // N_POOL_GROUPS
D_FF_DENSE = 2688
N_EXPERTS = 8
TOP_K = 2
D_FF_EXPERT = 1344
MOE_BLOCK = 256
RMS_EPS = 1e-6
N_MLA_LAYERS = (DEPTH + 1) // 2
N_POOL_LAYERS = DEPTH // 2

kernel_name = "hybrid_mla_pool_moe_decoder"


def rms_norm(x, g):
    xf = x.astype(jnp.float32)
    y = xf * lax.rsqrt(jnp.mean(xf * xf, axis=-1, keepdims=True) + RMS_EPS)
    return (y * g.astype(jnp.float32)).astype(x.dtype)


def rope_tables(positions):
    inv_freq = ROPE_THETA ** (-jnp.arange(0, QK_ROPE_DIM, 2, dtype=jnp.float32) / QK_ROPE_DIM)
    ang = positions.astype(jnp.float32)[..., None] * inv_freq
    return jnp.cos(ang), jnp.sin(ang)


def apply_rope(x, cos, sin):
    xf = x.astype(jnp.float32)
    x1, x2 = jnp.split(xf, 2, axis=-1)
    return jnp.concatenate([x1 * cos - x2 * sin, x2 * cos + x1 * sin], axis=-1).astype(x.dtype)


def mla(h, cos, sin, w_down, q_norm, w_uq, kv_norm, w_ukv, w_o):
    B, S, _ = h.shape
    down = h @ w_down
    c_q, c_kv, k_rope = jnp.split(down, [Q_LORA_RANK, Q_LORA_RANK + KV_LORA_RANK], axis=-1)
    q = (rms_norm(c_q, q_norm) @ w_uq).reshape(B, S, N_HEADS, QK_NOPE_DIM + QK_ROPE_DIM)
    q_nope, q_rope = q[..., :QK_NOPE_DIM], q[..., QK_NOPE_DIM:]
    q_rope = apply_rope(q_rope, cos[:, :, None, :], sin[:, :, None, :])
    k_rope = apply_rope(k_rope, cos, sin)
    kv = (rms_norm(c_kv, kv_norm) @ w_ukv).reshape(B, S, N_HEADS, QK_NOPE_DIM + V_HEAD_DIM)
    k_nope, v = kv[..., :QK_NOPE_DIM], kv[..., QK_NOPE_DIM:]
    scale = (QK_NOPE_DIM + QK_ROPE_DIM) ** -0.5
    q_nope = q_nope * scale
    q_rope = q_rope * scale
    outs = []
    for i in range(S // Q_BLOCK):
        q0, q1 = i * Q_BLOCK, (i + 1) * Q_BLOCK
        s = (jnp.einsum('bqhd,bkhd->bhqk', q_nope[:, q0:q1], k_nope[:, :q1])
             + jnp.einsum('bqhr,bkr->bhqk', q_rope[:, q0:q1], k_rope[:, :q1]))
        mask = jnp.arange(q0, q1)[:, None] >= jnp.arange(q1)[None, :]
        s = jnp.where(mask, s.astype(jnp.float32), -jnp.inf)
        p = jax.nn.softmax(s, axis=-1).astype(v.dtype)
        outs.append(jnp.einsum('bhqk,bkhd->bqhd', p, v[:, :q1]))
    o = jnp.concatenate(outs, axis=1).reshape(B, S, N_HEADS * V_HEAD_DIM)
    return o @ w_o


def multiscale_pool(h, pool_w, pool_scale):
    B, S, D = h.shape
    hf = h.astype(jnp.float32)
    t = jnp.arange(1, S + 1, dtype=jnp.float32)
    groups = []
    for g, w in enumerate(POOL_WINDOWS):
        hg = hf[..., g * POOL_GROUP_DIM:(g + 1) * POOL_GROUP_DIM]
        cs = jnp.cumsum(hg, axis=1)
        cs_lag = jnp.pad(cs[:, :S - w], ((0, 0), (w, 0), (0, 0)))
        cnt = jnp.minimum(t, float(w))[None, :, None]
        groups.append((cs - cs_lag) / cnt - hg)
    z = jnp.stack(groups, axis=2).astype(h.dtype)
    y = jnp.einsum('bsgc,gcd->bsgd', z, pool_w).reshape(B, S, D)
    return y * pool_scale


def swiglu(h, w_gate, w_up, w_down):
    return (jax.nn.silu(h @ w_gate) * (h @ w_up)) @ w_down


def moe_swiglu(h, router, w_gate, w_up, w_down):
    B, S, D = h.shape
    N = B * S
    NK = N * TOP_K
    xf = h.reshape(N, D)
    logits = (xf @ router).astype(jnp.float32)
    top_v, top_i = lax.top_k(logits, TOP_K)
    gates = jax.nn.softmax(top_v, axis=-1)
    flat_e = top_i.reshape(-1).astype(jnp.int32)
    flat_tok = jnp.arange(NK, dtype=jnp.int32) // TOP_K
    flat_g = gates.reshape(-1)
    order = jnp.argsort(flat_e)
    sorted_e = flat_e[order]
    counts = jnp.bincount(flat_e, length=N_EXPERTS).astype(jnp.int32)
    padded = (counts + MOE_BLOCK - 1) // MOE_BLOCK * MOE_BLOCK
    pad_end = jnp.cumsum(padded)
    pad_start = pad_end - padded
    start = jnp.cumsum(counts) - counts
    rank = jnp.arange(NK, dtype=jnp.int32) - start[sorted_e]
    dest = pad_start[sorted_e] + rank
    n_blocks = -(-NK // MOE_BLOCK) + N_EXPERTS
    P = n_blocks * MOE_BLOCK
    buf_tok = jnp.zeros((P,), jnp.int32).at[dest].set(flat_tok[order])
    buf_gate = jnp.zeros((P,), jnp.float32).at[dest].set(flat_g[order])
    block_e = jnp.clip(jnp.searchsorted(pad_end, jnp.arange(n_blocks, dtype=jnp.int32) * MOE_BLOCK,
                                        side='right'), 0, N_EXPERTS - 1).astype(jnp.int32)
    xb = xf[buf_tok].reshape(n_blocks, MOE_BLOCK, D)

    def expert_block(args):
        xblk, e = args
        return swiglu(xblk, w_gate[e], w_up[e], w_down[e])

    yb = lax.map(expert_block, (xb, block_e)).reshape(P, D)
    y = jnp.zeros((N, D), h.dtype).at[buf_tok].add(yb * buf_gate[:, None].astype(yb.dtype))
    return y.reshape(B, S, D)


def setup_inputs(seed: int = 0) -> dict:
    key = jax.random.key(seed)
    ks = iter(jax.random.split(key, 32))

    def w(shape, fan_in):
        return jax.random.normal(next(ks), shape, jnp.float32) * (fan_in ** -0.5)

    def gain(shape):
        return 1.0 + 0.05 * jax.random.normal(next(ks), shape, jnp.float32)

    LA, LB = N_MLA_LAYERS, N_POOL_LAYERS
    x = jax.random.normal(next(ks), (BATCH, SEQ, D_MODEL), jnp.float32)
    positions = (jnp.arange(SEQ, dtype=jnp.int32)[None, :]
                 + jax.random.randint(next(ks), (BATCH, 1), 0, 1024, dtype=jnp.int32))
    return {
        "x": x,
        "positions": positions,
        "attn_norm": gain((DEPTH, D_MODEL)),
        "ffn_norm": gain((DEPTH, D_MODEL)),
        "mla_w_down": w((LA, D_MODEL, Q_LORA_RANK + KV_LORA_RANK + QK_ROPE_DIM), D_MODEL),
        "mla_q_norm": gain((LA, Q_LORA_RANK)),
        "mla_w_uq": w((LA, Q_LORA_RANK, N_HEADS * (QK_NOPE_DIM + QK_ROPE_DIM)), Q_LORA_RANK),
        "mla_kv_norm": gain((LA, KV_LORA_RANK)),
        "mla_w_ukv": w((LA, KV_LORA_RANK, N_HEADS * (QK_NOPE_DIM + V_HEAD_DIM)), KV_LORA_RANK),
        "mla_w_o": w((LA, N_HEADS * V_HEAD_DIM, D_MODEL), N_HEADS * V_HEAD_DIM),
        "pool_w": w((LB, N_POOL_GROUPS, POOL_GROUP_DIM, POOL_GROUP_DIM), POOL_GROUP_DIM),
        "pool_scale": gain((LB, D_MODEL)),
        "ffn_w_gate": w((LA, D_MODEL, D_FF_DENSE), D_MODEL),
        "ffn_w_up": w((LA, D_MODEL, D_FF_DENSE), D_MODEL),
        "ffn_w_down": w((LA, D_FF_DENSE, D_MODEL), D_FF_DENSE),
        "moe_router": w((LB, D_MODEL, N_EXPERTS), D_MODEL),
        "moe_w_gate": w((LB, N_EXPERTS, D_MODEL, D_FF_EXPERT), D_MODEL),
        "moe_w_up": w((LB, N_EXPERTS, D_MODEL, D_FF_EXPERT), D_MODEL),
        "moe_w_down": w((LB, N_EXPERTS, D_FF_EXPERT, D_MODEL), D_FF_EXPERT),
        "final_norm": gain((D_MODEL,)),
    }


def reference(x, positions, attn_norm, ffn_norm, mla_w_down, mla_q_norm, mla_w_uq, mla_kv_norm,
              mla_w_ukv, mla_w_o, pool_w, pool_scale, ffn_w_gate, ffn_w_up, ffn_w_down,
              moe_router, moe_w_gate, moe_w_up, moe_w_down, final_norm):
    h = x
    cos, sin = rope_tables(positions)
    for layer in range(DEPTH):
        a = layer // 2
        hn = rms_norm(h, attn_norm[layer])
        if layer % N_MIXERS == 0:
            h = h + mla(hn, cos, sin, mla_w_down[a], mla_q_norm[a], mla_w_uq[a], mla_kv_norm[a],
                        mla_w_ukv[a], mla_w_o[a])
        else:
            h = h + multiscale_pool(hn, pool_w[a], pool_scale[a])
        hn = rms_norm(h, ffn_norm[layer])
        if layer % 2 == 0:
            h = h + swiglu(hn, ffn_w_gate[a], ffn_w_up[a], ffn_w_down[a])
        else:
            h = h + moe_swiglu(hn, moe_router[a], moe_w_gate[a], moe_w_up[a], moe_w_down[a])
    return rms_norm(h, final_norm)
```

```python
import functools

import jax
import jax.numpy as jnp
from jax import lax
from jax.experimental import pallas as pl
from jax.experimental.pallas import tpu as pltpu

F32 = jnp.float32
BF16 = jnp.bfloat16
I32 = jnp.int32

N_HEADS = 8
QK_NOPE_DIM = 128
QK_ROPE_DIM = 64
QK_DIM = QK_NOPE_DIM + QK_ROPE_DIM
V_HEAD_DIM = 128
Q_LORA_RANK = 512
KV_LORA_RANK = 256
ROPE_THETA = 10000.0
POOL_WINDOWS = (2, 4, 8, 16)
POOL_HALO = 16
N_EXPERTS = 8
RMS_EPS = 1e-6

LANES = 128
SUBLANES = 8
VMEM_LIMIT = 56 * 1024 * 1024

ROW_TILE = 512
ATTN_TILE = 512
ROUTER_TILE = 1024
MOE_ROWS = 512
DMA_TILE = 512
FF_LANE_PAD = 128


def _rms(x, g):
    ms = jnp.mean(x * x, axis=-1, keepdims=True)
    return x * lax.rsqrt(ms + RMS_EPS) * g


def _dot(a, b):
    return jnp.dot(a, b, preferred_element_type=F32)


def _dot_nt(a, b):
    return lax.dot_general(a, b, (((1,), (1,)), ((), ())), preferred_element_type=F32)


def _params(sem):
    return pltpu.CompilerParams(dimension_semantics=sem, vmem_limit_bytes=VMEM_LIMIT)


def _rows_to_tiles(ref, val):
    t = val.shape[0]
    for j in range(SUBLANES):
        ref[pl.ds(j, t, stride=SUBLANES), :] = val[:, j * LANES:(j + 1) * LANES]


def _tiles_to_rows(ref, t):
    return jnp.concatenate(
        [ref[pl.ds(j, t, stride=SUBLANES), :] for j in range(SUBLANES)], axis=-1)


def _rope_kernel(pos_ref, invf_ref, cos_ref, sin_ref):
    ang = pos_ref[...].astype(F32) * invf_ref[...]
    cos_ref[...] = jnp.cos(ang)
    sin_ref[...] = jnp.sin(ang)


def _rope_table(positions):
    n = positions.size
    half = QK_ROPE_DIM // 2
    per_row = LANES // half
    inv_freq = ROPE_THETA ** (-jnp.arange(0, QK_ROPE_DIM, 2, dtype=F32) / QK_ROPE_DIM)
    pos = jnp.repeat(positions.reshape(n // per_row, per_row), half, axis=1)
    invf = jnp.tile(inv_freq, per_row)[None, :]
    rows = n // per_row
    tr = min(rows, 2048)
    cos, sin = pl.pallas_call(
        _rope_kernel,
        grid=(rows // tr,),
        in_specs=[pl.BlockSpec((tr, LANES), lambda i: (i, 0)),
                  pl.BlockSpec((1, LANES), lambda i: (0, 0))],
        out_specs=[pl.BlockSpec((tr, LANES), lambda i: (i, 0)),
                   pl.BlockSpec((tr, LANES), lambda i: (i, 0))],
        out_shape=[jax.ShapeDtypeStruct((rows, LANES), F32)] * 2,
        compiler_params=_params(("arbitrary",)),
        name="rope_table",
    )(pos, invf)
    cos = cos.reshape(n, half)
    sin = sin.reshape(n, half)
    return jnp.concatenate([cos, cos, sin, sin], axis=-1)


def _mla_proj_kernel(h_ref, g_ref, wd_ref, qn_ref, wuq_ref, kvn_ref, wukv_ref, cs_ref,
                     q_ref, k_ref, v_ref):
    hn = _rms(h_ref[...], g_ref[...]).astype(BF16)
    down = _dot(hn, wd_ref[...])
    kv0 = Q_LORA_RANK
    kr0 = Q_LORA_RANK + KV_LORA_RANK
    cq = _rms(down[:, :kv0], qn_ref[...]).astype(BF16)
    ckv = _rms(down[:, kv0:kr0], kvn_ref[...]).astype(BF16)
    cs = cs_ref[...]
    cos2 = cs[:, :QK_ROPE_DIM]
    sin2 = cs[:, QK_ROPE_DIM:]
    k_rope = (down[:, kr0:kr0 + QK_ROPE_DIM] * cos2
              + down[:, kr0 + QK_ROPE_DIM:kr0 + 2 * QK_ROPE_DIM] * sin2).astype(BF16)
    q = _dot(cq, wuq_ref[...])
    kv = _dot(ckv, wukv_ref[...])
    scale = QK_DIM ** -0.5
    rope0 = N_HEADS * QK_NOPE_DIM
    rot0 = rope0 + N_HEADS * QK_ROPE_DIM
    v0 = N_HEADS * QK_NOPE_DIM
    for hd in range(N_HEADS):
        q_nope = q[:, hd * QK_NOPE_DIM:(hd + 1) * QK_NOPE_DIM] * scale
        q_rope = (q[:, rope0 + hd * QK_ROPE_DIM:rope0 + (hd + 1) * QK_ROPE_DIM] * cos2
                  + q[:, rot0 + hd * QK_ROPE_DIM:rot0 + (hd + 1) * QK_ROPE_DIM] * sin2) * scale
        q_ref[0, hd, :, 0:QK_NOPE_DIM] = q_nope.astype(BF16)
        q_ref[0, hd, :, QK_NOPE_DIM:QK_DIM] = q_rope.astype(BF16)
        k_ref[0, hd, :, 0:QK_NOPE_DIM] = kv[:, hd * QK_NOPE_DIM:(hd + 1) * QK_NOPE_DIM].astype(BF16)
        k_ref[0, hd, :, QK_NOPE_DIM:QK_DIM] = k_rope
        v_ref[0, hd] = kv[:, v0 + hd * V_HEAD_DIM:v0 + (hd + 1) * V_HEAD_DIM].astype(BF16)


def _half_rotated(w):
    half = w.shape[-1] // 2
    return jnp.concatenate([-w[..., half:], w[..., :half]], axis=-1)


def _prep_mla_weights(w_down, w_uq, w_ukv):
    d = w_down.shape[0]
    kr0 = Q_LORA_RANK + KV_LORA_RANK
    wd = jnp.concatenate([w_down, _half_rotated(w_down[:, kr0:])], axis=1).astype(BF16)
    wq = w_uq.reshape(Q_LORA_RANK, N_HEADS, QK_DIM)
    wq_nope = wq[:, :, :QK_NOPE_DIM].reshape(Q_LORA_RANK, -1)
    wq_rope = wq[:, :, QK_NOPE_DIM:]
    wuq = jnp.concatenate([wq_nope, wq_rope.reshape(Q_LORA_RANK, -1),
                           _half_rotated(wq_rope).reshape(Q_LORA_RANK, -1)], axis=1).astype(BF16)
    wkv = w_ukv.reshape(KV_LORA_RANK, N_HEADS, QK_NOPE_DIM + V_HEAD_DIM)
    wukv = jnp.concatenate([wkv[:, :, :QK_NOPE_DIM].reshape(KV_LORA_RANK, -1),
                            wkv[:, :, QK_NOPE_DIM:].reshape(KV_LORA_RANK, -1)], axis=1).astype(BF16)
    del d
    return wd, wuq, wukv


def _mla_proj(h, g, wd, q_norm, wuq, kv_norm, wukv, cs, batch, seq):
    n, d = h.shape
    tm = min(ROW_TILE, seq)
    per_b = seq // tm
    full = lambda a: pl.BlockSpec(a.shape, lambda i: (0,) * a.ndim)
    head_spec = lambda w: pl.BlockSpec((1, N_HEADS, tm, w), lambda i: (i // per_b, 0, i % per_b, 0))
    g2, qn2, kvn2 = g[None, :], q_norm[None, :], kv_norm[None, :]
    return pl.pallas_call(
        _mla_proj_kernel,
        grid=(n // tm,),
        in_specs=[pl.BlockSpec((tm, d), lambda i: (i, 0)), full(g2), full(wd), full(qn2),
                  full(wuq), full(kvn2), full(wukv),
                  pl.BlockSpec((tm, LANES), lambda i: (i, 0))],
        out_specs=[head_spec(QK_DIM), head_spec(QK_DIM), head_spec(V_HEAD_DIM)],
        out_shape=[jax.ShapeDtypeStruct((batch, N_HEADS, seq, QK_DIM), BF16),
                   jax.ShapeDtypeStruct((batch, N_HEADS, seq, QK_DIM), BF16),
                   jax.ShapeDtypeStruct((batch, N_HEADS, seq, V_HEAD_DIM), BF16)],
        compiler_params=_params(("arbitrary",)),
        name="mla_proj",
    )(h, g2, wd, qn2, wuq, kvn2, wukv, cs)


def _attn_kernel(q_ref, k_ref, v_ref, o_ref, m_ref, l_ref, acc_ref):
    qi = pl.program_id(2)
    ki = pl.program_id(3)

    @pl.when(ki == 0)
    def _():
        m_ref[...] = jnp.full(m_ref.shape, -jnp.inf, F32)
        l_ref[...] = jnp.zeros(l_ref.shape, F32)
        acc_ref[...] = jnp.zeros(acc_ref.shape, F32)

    @pl.when(ki <= qi)
    def _():
        s = _dot_nt(q_ref[0, 0], k_ref[0, 0])
        row = lax.broadcasted_iota(I32, s.shape, 0)
        col = lax.broadcasted_iota(I32, s.shape, 1)
        s = jnp.where((row >= col) | (ki < qi), s, -jnp.inf)
        m_prev = m_ref[...]
        m_new = jnp.maximum(m_prev, jnp.max(s, axis=1, keepdims=True))
        p = jnp.exp(s - m_new)
        alpha = jnp.exp(m_prev - m_new)
        l_ref[...] = alpha * l_ref[...] + jnp.sum(p, axis=1, keepdims=True)
        acc_ref[...] = alpha * acc_ref[...] + _dot(p.astype(BF16), v_ref[0, 0])
        m_ref[...] = m_new

    @pl.when(ki == qi)
    def _():
        o_ref[0] = (acc_ref[...] / l_ref[...]).astype(o_ref.dtype)


def _attention(q, k, v):
    b, nh, s, _ = q.shape
    t = min(ATTN_TILE, s)
    nt = s // t
    kv_idx = lambda bi, hi, qi, ki: (bi, hi, jnp.minimum(ki, qi), 0)
    return pl.pallas_call(
        _attn_kernel,
        grid=(b, nh, nt, nt),
        in_specs=[pl.BlockSpec((1, 1, t, QK_DIM), lambda bi, hi, qi, ki: (bi, hi, qi, 0)),
                  pl.BlockSpec((1, 1, t, QK_DIM), kv_idx),
                  pl.BlockSpec((1, 1, t, V_HEAD_DIM), kv_idx)],
        out_specs=pl.BlockSpec((1, t, V_HEAD_DIM), lambda bi, hi, qi, ki: (bi, qi, hi)),
        out_shape=jax.ShapeDtypeStruct((b, s, nh * V_HEAD_DIM), BF16),
        scratch_shapes=[pltpu.VMEM((t, 1), F32), pltpu.VMEM((t, 1), F32),
                        pltpu.VMEM((t, V_HEAD_DIM), F32)],
        compiler_params=_params(("arbitrary",) * 4),
        name="mla_attention",
    )(q, k, v)


def _oproj_ffn_kernel(h_ref, o_ref, wo_ref, g_ref, wg_ref, wu_ref, wd_ref, out_ref, *, chunk):
    h1 = h_ref[...] + _dot(o_ref[...], wo_ref[...])
    hn = _rms(h1, g_ref[...]).astype(BF16)
    acc = h1
    for c0 in range(0, wg_ref.shape[1], chunk):
        gate = _dot(hn, wg_ref[:, c0:c0 + chunk])
        up = _dot(hn, wu_ref[:, c0:c0 + chunk])
        act = (gate * jax.nn.sigmoid(gate) * up).astype(BF16)
        acc = acc + _dot(act, wd_ref[c0:c0 + chunk, :])
    out_ref[...] = acc


def _ff_chunk(ff):
    for c in (896, 768, 640, 512, 384, 256, 128):
        if ff % c == 0:
            return c
    return ff


def _oproj_ffn(h, o, wo, g, wg, wu, wd):
    n, d = h.shape
    tm = min(ROW_TILE, n)
    full = lambda a: pl.BlockSpec(a.shape, lambda i: (0,) * a.ndim)
    g2 = g[None, :]
    return pl.pallas_call(
        functools.partial(_oproj_ffn_kernel, chunk=_ff_chunk(wg.shape[1])),
        grid=(n // tm,),
        in_specs=[pl.BlockSpec((tm, d), lambda i: (i, 0)),
                  pl.BlockSpec((tm, o.shape[1]), lambda i: (i, 0)),
                  full(wo), full(g2), full(wg), full(wu), full(wd)],
        out_specs=pl.BlockSpec((tm, d), lambda i: (i, 0)),
        out_shape=jax.ShapeDtypeStruct((n, d), F32),
        compiler_params=_params(("arbitrary",)),
        name="oproj_ffn",
    )(h, o, wo, g2, wg, wu, wd)


def _pool_kernel(h_ref, g_ref, w_ref, sc_ref, out_ref, pad_ref):
    s, d = h_ref.shape[1], h_ref.shape[2]
    gd = d // len(POOL_WINDOWS)
    h = h_ref[0]
    hn = _rms(h, g_ref[...])
    t1 = lax.broadcasted_iota(I32, (s, 1), 0) + 1
    pad_ref[0:POOL_HALO, :] = jnp.zeros((POOL_HALO, gd), F32)
    for gi, w in enumerate(POOL_WINDOWS):
        x = hn[:, gi * gd:(gi + 1) * gd]
        acc = x
        k = 1
        while k < w:
            pad_ref[POOL_HALO:POOL_HALO + s, :] = acc
            acc = acc + pad_ref[POOL_HALO - k:POOL_HALO - k + s, :]
            k *= 2
        cnt = jnp.minimum(t1, w).astype(F32)
        z = (acc / cnt - x).astype(BF16)
        y = _dot(z, w_ref[gi]) * sc_ref[:, gi * gd:(gi + 1) * gd]
        out_ref[0, :, gi * gd:(gi + 1) * gd] = h[:, gi * gd:(gi + 1) * gd] + y


def _pool(h3, g, pool_w, pool_scale):
    b, s, d = h3.shape
    gd = d // len(POOL_WINDOWS)
    full = lambda a: pl.BlockSpec(a.shape, lambda i: (0,) * a.ndim)
    g2, sc2 = g[None, :], pool_scale[None, :]
    return pl.pallas_call(
        _pool_kernel,
        grid=(b,),
        in_specs=[pl.BlockSpec((1, s, d), lambda i: (i, 0, 0)), full(g2), full(pool_w), full(sc2)],
        out_specs=pl.BlockSpec((1, s, d), lambda i: (i, 0, 0)),
        out_shape=jax.ShapeDtypeStruct((b, s, d), F32),
        scratch_shapes=[pltpu.VMEM((s + POOL_HALO, gd), F32)],
        compiler_params=_params(("arbitrary",)),
        name="pool_mixer",
    )(h3, g2, pool_w, sc2)


def _router_kernel(h_ref, g_ref, rt_ref, tri_ref, xt_ref, meta_ref, gate_ref, cnt_ref, carry_ref):
    t = h_ref.shape[0]

    @pl.when(pl.program_id(0) == 0)
    def _():
        carry_ref[...] = jnp.zeros(carry_ref.shape, F32)

    hn = _rms(h_ref[...], g_ref[...])
    _rows_to_tiles(xt_ref, hn)

    hi = hn.astype(BF16)
    lo = (hn - hi.astype(F32)).astype(BF16)
    rt = rt_ref[...]
    a = _dot_nt(rt, hi)
    logits = a[:N_EXPERTS] + a[N_EXPERTS:] + _dot_nt(rt[:N_EXPERTS], lo)

    eidx = lax.broadcasted_iota(I32, logits.shape, 0)
    m1 = jnp.max(logits, axis=0, keepdims=True)
    i1 = jnp.min(jnp.where(logits == m1, eidx, N_EXPERTS), axis=0, keepdims=True)
    oh1 = eidx == i1
    rest = jnp.where(oh1, -jnp.inf, logits)
    m2 = jnp.max(rest, axis=0, keepdims=True)
    i2 = jnp.min(jnp.where(rest == m2, eidx, N_EXPERTS), axis=0, keepdims=True)
    oh2 = eidx == i2
    e2 = jnp.exp(m2 - m1)
    den = 1.0 + e2
    g1 = 1.0 / den
    g2 = e2 / den

    sel = jnp.where(oh1 | oh2, 1.0, 0.0).astype(F32)
    pref = _dot(sel.astype(BF16), tri_ref[...])
    carry = carry_ref[:, 0:1]
    base = carry + pref
    r1 = jnp.sum(jnp.where(oh1, base, 0.0), axis=0, keepdims=True)
    r2 = jnp.sum(jnp.where(oh2, base, 0.0), axis=0, keepdims=True)
    total = carry + jnp.sum(sel, axis=1, keepdims=True)
    carry_ref[...] = jnp.broadcast_to(total, carry_ref.shape)
    cnt_ref[...] = jnp.broadcast_to(total, cnt_ref.shape).astype(I32)

    zi = jnp.zeros((SUBLANES - 4, t), I32)
    meta_ref[...] = jnp.concatenate([i1, i2, r1.astype(I32), r2.astype(I32), zi], axis=0)
    zf = jnp.zeros((SUBLANES - 2, t), F32)
    gate_ref[...] = jnp.concatenate([g1, g2, zf], axis=0)


def _router(h, g, router):
    n, d = h.shape
    t = min(ROUTER_TILE, n)
    r_t = router.T
    r_hi = r_t.astype(BF16)
    r_lo = (r_t - r_hi.astype(F32)).astype(BF16)
    rt = jnp.concatenate([r_hi, r_lo], axis=0)
    tri = jnp.triu(jnp.ones((t, t), BF16), k=1)
    full = lambda a: pl.BlockSpec(a.shape, lambda i: (0,) * a.ndim)
    g2 = g[None, :]
    return pl.pallas_call(
        _router_kernel,
        grid=(n // t,),
        in_specs=[pl.BlockSpec((t, d), lambda i: (i, 0)), full(g2), full(rt), full(tri)],
        out_specs=[pl.BlockSpec((t * SUBLANES, LANES), lambda i: (i, 0)),
                   pl.BlockSpec((SUBLANES, t), lambda i: (0, i)),
                   pl.BlockSpec((SUBLANES, t), lambda i: (0, i)),
                   pl.BlockSpec((N_EXPERTS, LANES), lambda i: (0, 0))],
        out_shape=[jax.ShapeDtypeStruct((n * SUBLANES, LANES), F32),
                   jax.ShapeDtypeStruct((SUBLANES, n), I32),
                   jax.ShapeDtypeStruct((SUBLANES, n), F32),
                   jax.ShapeDtypeStruct((N_EXPERTS, LANES), I32)],
        scratch_shapes=[pltpu.VMEM((N_EXPERTS, LANES), F32)],
        compiler_params=_params(("arbitrary",)),
        name="moe_router",
    )(h, g2, rt, tri)


def _row(ref, r):
    return ref.at[pl.ds(pl.multiple_of(r * SUBLANES, SUBLANES), SUBLANES)]


def _dispatch_kernel(d1_ref, d2_ref, zb_ref, xt_ref, xb_ref, zero_ref, sem, *, td, n_zero):
    i = pl.program_id(0)
    last = pl.num_programs(0) - 1

    def wait_rows(n_rows):
        span = xb_ref.at[pl.ds(0, n_rows * SUBLANES)]
        pltpu.make_async_copy(span, span, sem).wait()

    @pl.when(i == 0)
    def _():
        zero_ref[...] = jnp.zeros(zero_ref.shape, F32)
        for e in range(N_EXPERTS):
            def zbody(r, c):
                pltpu.make_async_copy(zero_ref, _row(xb_ref, r), sem).start()
                return c
            lax.fori_loop(zb_ref[0, e], zb_ref[1, e], zbody, 0)

    def body(j, c):
        src = _row(xt_ref, i * td + j)
        pltpu.make_async_copy(src, _row(xb_ref, d1_ref[0, 0, j]), sem).start()
        pltpu.make_async_copy(src, _row(xb_ref, d2_ref[0, 0, j]), sem).start()
        return c
    lax.fori_loop(0, td, body, 0)

    @pl.when(i > 0)
    def _():
        wait_rows(2 * td)

    @pl.when(i == last)
    def _():
        wait_rows(2 * td)
        wait_rows(n_zero)


def _dispatch(xt, d1, d2, zero_bounds, n_rows_out):
    n = xt.shape[0] // SUBLANES
    td = min(DMA_TILE, n)
    steps = n // td
    n_zero = n_rows_out - 2 * n
    smem_tile = pl.BlockSpec((1, 1, td), lambda i: (i, 0, 0), memory_space=pltpu.SMEM)
    return pl.pallas_call(
        functools.partial(_dispatch_kernel, td=td, n_zero=n_zero),
        grid=(steps,),
        in_specs=[smem_tile, smem_tile,
                  pl.BlockSpec(memory_space=pltpu.SMEM),
                  pl.BlockSpec(memory_space=pl.ANY)],
        out_specs=pl.BlockSpec(memory_space=pl.ANY),
        out_shape=jax.ShapeDtypeStruct((n_rows_out * SUBLANES, LANES), F32),
        scratch_shapes=[pltpu.VMEM((SUBLANES, LANES), F32), pltpu.SemaphoreType.DMA],
        compiler_params=_params(("arbitrary",)),
        name="moe_dispatch",
    )(d1.reshape(steps, 1, td), d2.reshape(steps, 1, td), zero_bounds, xt)


def _expert_kernel(be_ref, x_ref, wg_ref, wu_ref, wd_ref, y_ref):
    del be_ref
    bm = x_ref.shape[0] // SUBLANES
    x = _tiles_to_rows(x_ref, bm).astype(BF16)
    gate = _dot(x, wg_ref[0])
    up = _dot(x, wu_ref[0])
    act = (gate * jax.nn.sigmoid(gate) * up).astype(BF16)
    _rows_to_tiles(y_ref, _dot(act, wd_ref[0]))


def _experts(xb, block_e, wg, wu, wd):
    bm = MOE_ROWS
    n_blocks = xb.shape[0] // (bm * SUBLANES)
    d, ff = wg.shape[1], wg.shape[2]
    return pl.pallas_call(
        _expert_kernel,
        grid_spec=pltpu.PrefetchScalarGridSpec(
            num_scalar_prefetch=1,
            grid=(n_blocks,),
            in_specs=[pl.BlockSpec((bm * SUBLANES, LANES), lambda i, be: (i, 0)),
                      pl.BlockSpec((1, d, ff), lambda i, be: (be[i], 0, 0)),
                      pl.BlockSpec((1, d, ff), lambda i, be: (be[i], 0, 0)),
                      pl.BlockSpec((1, ff, d), lambda i, be: (be[i], 0, 0))],
            out_specs=pl.BlockSpec((bm * SUBLANES, LANES), lambda i, be: (i, 0)),
        ),
        out_shape=jax.ShapeDtypeStruct(xb.shape, F32),
        compiler_params=_params(("arbitrary",)),
        name="moe_experts",
    )(block_e, xb, wg, wu, wd)


def _combine_kernel(d1_ref, d2_ref, h_ref, gc_ref, fn_ref, yb_ref, out_ref, b1_ref, b2_ref, sem,
                    *, final_norm):
    t = h_ref.shape[0]

    def body(j, c):
        pltpu.make_async_copy(_row(yb_ref, d1_ref[0, 0, j]), _row(b1_ref, j), sem).start()
        pltpu.make_async_copy(_row(yb_ref, d2_ref[0, 0, j]), _row(b2_ref, j), sem).start()
        return c
    lax.fori_loop(0, t, body, 0)
    pltpu.make_async_copy(yb_ref.at[pl.ds(0, t * SUBLANES)], b1_ref, sem).wait()
    pltpu.make_async_copy(yb_ref.at[pl.ds(0, t * SUBLANES)], b2_ref, sem).wait()

    gc = gc_ref[...]
    out = (h_ref[...] + gc[:, 0:1] * _tiles_to_rows(b1_ref, t)
           + gc[:, 1:2] * _tiles_to_rows(b2_ref, t))
    if final_norm:
        out = _rms(out, fn_ref[...])
    out_ref[...] = out


def _combine(h, yb, d1, d2, gcol, fnorm, final_norm):
    n, d = h.shape
    t = min(DMA_TILE, n)
    steps = n // t
    smem_tile = pl.BlockSpec((1, 1, t), lambda i: (i, 0, 0), memory_space=pltpu.SMEM)
    fn2 = fnorm[None, :]
    return pl.pallas_call(
        functools.partial(_combine_kernel, final_norm=final_norm),
        grid=(steps,),
        in_specs=[smem_tile, smem_tile,
                  pl.BlockSpec((t, d), lambda i: (i, 0)),
                  pl.BlockSpec((t, gcol.shape[1]), lambda i: (i, 0)),
                  pl.BlockSpec(fn2.shape, lambda i: (0, 0)),
                  pl.BlockSpec(memory_space=pl.ANY)],
        out_specs=pl.BlockSpec((t, d), lambda i: (i, 0)),
        out_shape=jax.ShapeDtypeStruct((n, d), F32),
        scratch_shapes=[pltpu.VMEM((t * SUBLANES, LANES), F32),
                        pltpu.VMEM((t * SUBLANES, LANES), F32),
                        pltpu.SemaphoreType.DMA],
        compiler_params=_params(("arbitrary",)),
        name="moe_combine",
    )(d1.reshape(steps, 1, t), d2.reshape(steps, 1, t), h, gcol, fn2, yb)


def _moe(h, g, router, wg, wu, wd, fnorm, final_norm):
    n = h.shape[0]
    bm = MOE_ROWS
    xt, meta, gates, cnt = _router(h, g, router)
    counts = cnt[:, 0]
    padded = (counts + bm - 1) // bm * bm
    pad_end = jnp.cumsum(padded)
    pad_start = pad_end - padded
    d1 = pad_start[meta[0]] + meta[2]
    d2 = pad_start[meta[1]] + meta[3]
    n_blocks = (n * 2) // bm + N_EXPERTS
    n_rows = n_blocks * bm
    zero_lo = pad_start + counts
    zero_hi = jnp.concatenate([pad_start[1:], jnp.array([n_rows], I32)])
    zero_bounds = jnp.stack([zero_lo, zero_hi]).astype(I32)
    block_e = jnp.clip(jnp.searchsorted(pad_end, jnp.arange(n_blocks, dtype=I32) * bm, side="right"),
                       0, N_EXPERTS - 1).astype(I32)
    xb = _dispatch(xt, d1, d2, zero_bounds, n_rows)
    yb = _experts(xb, block_e, wg, wu, wd)
    gcol = gates[:2].T
    return _combine(h, yb, d1, d2, gcol, fnorm, final_norm)


def _pad_ff(w, axis):
    ff = w.shape[axis]
    pad = (-ff) % FF_LANE_PAD
    if pad == 0:
        return w
    widths = [(0, 0)] * w.ndim
    widths[axis] = (0, pad)
    return jnp.pad(w, widths)


def kernel(x, positions, attn_norm, ffn_norm, mla_w_down, mla_q_norm, mla_w_uq, mla_kv_norm,
           mla_w_ukv, mla_w_o, pool_w, pool_scale, ffn_w_gate, ffn_w_up, ffn_w_down,
           moe_router, moe_w_gate, moe_w_up, moe_w_down, final_norm):
    b, s, d = x.shape
    n = b * s
    depth = attn_norm.shape[0]
    h = x.reshape(n, d)
    cs = _rope_table(positions)
    for layer in range(depth):
        a = layer // 2
        last = layer == depth - 1
        if layer % 2 == 0:
            wd, wuq, wukv = _prep_mla_weights(mla_w_down[a], mla_w_uq[a], mla_w_ukv[a])
            q, k, v = _mla_proj(h, attn_norm[layer], wd, mla_q_norm[a], wuq, mla_kv_norm[a],
                                wukv, cs, b, s)
            o = _attention(q, k, v).reshape(n, N_HEADS * V_HEAD_DIM)
            h = _oproj_ffn(h, o, mla_w_o[a].astype(BF16), ffn_norm[layer],
                           ffn_w_gate[a].astype(BF16), ffn_w_up[a].astype(BF16),
                           ffn_w_down[a].astype(BF16))
            if last:
                h = _final_norm(h, final_norm)
        else:
            h = _pool(h.reshape(b, s, d), attn_norm[layer], pool_w[a].astype(BF16),
                      pool_scale[a]).reshape(n, d)
            h = _moe(h, ffn_norm[layer], moe_router[a],
                     _pad_ff(moe_w_gate[a], 2).astype(BF16), _pad_ff(moe_w_up[a], 2).astype(BF16),
                     _pad_ff(moe_w_down[a], 1).astype(BF16), final_norm, last)
    return h.reshape(b, s, d)


def _final_norm_kernel(h_ref, g_ref, o_ref):
    o_ref[...] = _rms(h_ref[...], g_ref[...])


def _final_norm(h, g):
    n, d = h.shape
    tm = min(ROW_TILE, n)
    g2 = g[None, :]
    return pl.pallas_call(
        _final_norm_kernel,
        grid=(n // tm,),
        in_specs=[pl.BlockSpec((tm, d), lambda i: (i, 0)), pl.BlockSpec(g2.shape, lambda i: (0, 0))],
        out_specs=pl.BlockSpec((tm, d), lambda i: (i, 0)),
        out_shape=jax.ShapeDtypeStruct((n, d), F32),
        compiler_params=_params(("arbitrary",)),
        name="final_norm",
    )(h, g2)
```

```python
import functools

import jax
import jax.numpy as jnp
from jax import lax
from jax.experimental import pallas as pl
from jax.experimental.pallas import tpu as pltpu

F32 = jnp.float32
BF16 = jnp.bfloat16
I32 = jnp.int32

N_HEADS = 8
QK_NOPE_DIM = 128
QK_ROPE_DIM = 64
QK_DIM = QK_NOPE_DIM + QK_ROPE_DIM
V_HEAD_DIM = 128
Q_LORA_RANK = 512
KV_LORA_RANK = 256
ROPE_THETA = 10000.0
POOL_WINDOWS = (2, 4, 8, 16)
POOL_HALO = 16
N_EXPERTS = 8
RMS_EPS = 1e-6

LANES = 128
SUBLANES = 8
VMEM_LIMIT = 56 * 1024 * 1024

ROW_TILE = 512
ATTN_TILE = 512
ATTN_HEADS = 8
ROUTER_TILE = 1024
MOE_ROWS = 512
DMA_TILE = 512
DISPATCH_TILE = 2048
FF_LANE_PAD = 128


def _rms(x, g):
    ms = jnp.mean(x * x, axis=-1, keepdims=True)
    return x * lax.rsqrt(ms + RMS_EPS) * g


def _dot(a, b):
    return jnp.dot(a, b, preferred_element_type=F32)


def _dot_nt(a, b):
    return lax.dot_general(a, b, (((1,), (1,)), ((), ())), preferred_element_type=F32)


def _params(sem):
    return pltpu.CompilerParams(dimension_semantics=sem, vmem_limit_bytes=VMEM_LIMIT)


def _rows_to_tiles(ref, val):
    t = val.shape[0]
    for j in range(SUBLANES):
        ref[pl.ds(j, t, stride=SUBLANES), :] = val[:, j * LANES:(j + 1) * LANES]


def _tiles_to_rows(ref, t):
    return jnp.concatenate(
        [ref[pl.ds(j, t, stride=SUBLANES), :] for j in range(SUBLANES)], axis=-1)


def _rope_kernel(pos_ref, invf_ref, cos_ref, sin_ref):
    ang = pos_ref[...].astype(F32) * invf_ref[...]
    cos_ref[...] = jnp.cos(ang)
    sin_ref[...] = jnp.sin(ang)


def _rope_table(positions):
    n = positions.size
    half = QK_ROPE_DIM // 2
    per_row = LANES // half
    inv_freq = ROPE_THETA ** (-jnp.arange(0, QK_ROPE_DIM, 2, dtype=F32) / QK_ROPE_DIM)
    pos = jnp.repeat(positions.reshape(n // per_row, per_row), half, axis=1)
    invf = jnp.tile(inv_freq, per_row)[None, :]
    rows = n // per_row
    tr = min(rows, 2048)
    cos, sin = pl.pallas_call(
        _rope_kernel,
        grid=(rows // tr,),
        in_specs=[pl.BlockSpec((tr, LANES), lambda i: (i, 0)),
                  pl.BlockSpec((1, LANES), lambda i: (0, 0))],
        out_specs=[pl.BlockSpec((tr, LANES), lambda i: (i, 0)),
                   pl.BlockSpec((tr, LANES), lambda i: (i, 0))],
        out_shape=[jax.ShapeDtypeStruct((rows, LANES), F32)] * 2,
        compiler_params=_params(("arbitrary",)),
        name="rope_table",
    )(pos, invf)
    cos = cos.reshape(n, half)
    sin = sin.reshape(n, half)
    return jnp.concatenate([cos, cos, sin, sin], axis=-1)


def _mla_proj_kernel(h_ref, g_ref, wd_ref, qn_ref, wuq_ref, kvn_ref, wukv_ref, cs_ref,
                     q_ref, k_ref, v_ref):
    hn = _rms(h_ref[...], g_ref[...]).astype(BF16)
    down = _dot(hn, wd_ref[...])
    kv0 = Q_LORA_RANK
    kr0 = Q_LORA_RANK + KV_LORA_RANK
    cq = _rms(down[:, :kv0], qn_ref[...]).astype(BF16)
    ckv = _rms(down[:, kv0:kr0], kvn_ref[...]).astype(BF16)
    cs = cs_ref[...]
    cos2 = cs[:, :QK_ROPE_DIM]
    sin2 = cs[:, QK_ROPE_DIM:]
    k_rope = (down[:, kr0:kr0 + QK_ROPE_DIM] * cos2
              + down[:, kr0 + QK_ROPE_DIM:kr0 + 2 * QK_ROPE_DIM] * sin2).astype(BF16)
    q = _dot(cq, wuq_ref[...])
    kv = _dot(ckv, wukv_ref[...])
    scale = QK_DIM ** -0.5
    rope0 = N_HEADS * QK_NOPE_DIM
    rot0 = rope0 + N_HEADS * QK_ROPE_DIM
    v0 = N_HEADS * QK_NOPE_DIM
    for hd in range(N_HEADS):
        q_nope = q[:, hd * QK_NOPE_DIM:(hd + 1) * QK_NOPE_DIM] * scale
        q_rope = (q[:, rope0 + hd * QK_ROPE_DIM:rope0 + (hd + 1) * QK_ROPE_DIM] * cos2
                  + q[:, rot0 + hd * QK_ROPE_DIM:rot0 + (hd + 1) * QK_ROPE_DIM] * sin2) * scale
        q_ref[0, hd, :, 0:QK_NOPE_DIM] = q_nope.astype(BF16)
        q_ref[0, hd, :, QK_NOPE_DIM:QK_DIM] = q_rope.astype(BF16)
        k_ref[0, hd, :, 0:QK_NOPE_DIM] = kv[:, hd * QK_NOPE_DIM:(hd + 1) * QK_NOPE_DIM].astype(BF16)
        k_ref[0, hd, :, QK_NOPE_DIM:QK_DIM] = k_rope
        v_ref[0, hd] = kv[:, v0 + hd * V_HEAD_DIM:v0 + (hd + 1) * V_HEAD_DIM].astype(BF16)


def _half_rotated(w):
    half = w.shape[-1] // 2
    return jnp.concatenate([-w[..., half:], w[..., :half]], axis=-1)


def _prep_mla_weights(w_down, w_uq, w_ukv):
    d = w_down.shape[0]
    kr0 = Q_LORA_RANK + KV_LORA_RANK
    wd = jnp.concatenate([w_down, _half_rotated(w_down[:, kr0:])], axis=1).astype(BF16)
    wq = w_uq.reshape(Q_LORA_RANK, N_HEADS, QK_DIM)
    wq_nope = wq[:, :, :QK_NOPE_DIM].reshape(Q_LORA_RANK, -1)
    wq_rope = wq[:, :, QK_NOPE_DIM:]
    wuq = jnp.concatenate([wq_nope, wq_rope.reshape(Q_LORA_RANK, -1),
                           _half_rotated(wq_rope).reshape(Q_LORA_RANK, -1)], axis=1).astype(BF16)
    wkv = w_ukv.reshape(KV_LORA_RANK, N_HEADS, QK_NOPE_DIM + V_HEAD_DIM)
    wukv = jnp.concatenate([wkv[:, :, :QK_NOPE_DIM].reshape(KV_LORA_RANK, -1),
                            wkv[:, :, QK_NOPE_DIM:].reshape(KV_LORA_RANK, -1)], axis=1).astype(BF16)
    del d
    return wd, wuq, wukv


def _mla_proj(h, g, wd, q_norm, wuq, kv_norm, wukv, cs, batch, seq):
    n, d = h.shape
    tm = min(ROW_TILE, seq)
    per_b = seq // tm
    full = lambda a: pl.BlockSpec(a.shape, lambda i: (0,) * a.ndim)
    head_spec = lambda w: pl.BlockSpec((1, N_HEADS, tm, w), lambda i: (i // per_b, 0, i % per_b, 0))
    g2, qn2, kvn2 = g[None, :], q_norm[None, :], kv_norm[None, :]
    return pl.pallas_call(
        _mla_proj_kernel,
        grid=(n // tm,),
        in_specs=[pl.BlockSpec((tm, d), lambda i: (i, 0)), full(g2), full(wd), full(qn2),
                  full(wuq), full(kvn2), full(wukv),
                  pl.BlockSpec((tm, LANES), lambda i: (i, 0))],
        out_specs=[head_spec(QK_DIM), head_spec(QK_DIM), head_spec(V_HEAD_DIM)],
        out_shape=[jax.ShapeDtypeStruct((batch, N_HEADS, seq, QK_DIM), BF16),
                   jax.ShapeDtypeStruct((batch, N_HEADS, seq, QK_DIM), BF16),
                   jax.ShapeDtypeStruct((batch, N_HEADS, seq, V_HEAD_DIM), BF16)],
        compiler_params=_params(("arbitrary",)),
        name="mla_proj",
    )(h, g2, wd, qn2, wuq, kvn2, wukv, cs)


def _attn_kernel(q_ref, k_ref, v_ref, o_ref, m_ref, l_ref, acc_ref):
    hb, t = q_ref.shape[1], q_ref.shape[2]
    qi = pl.program_id(2)
    m_ref[...] = jnp.full(m_ref.shape, -jnp.inf, F32)
    l_ref[...] = jnp.zeros(l_ref.shape, F32)
    acc_ref[...] = jnp.zeros(acc_ref.shape, F32)

    def kv_tile(ki, diagonal):
        k0 = pl.multiple_of(ki * t, t)
        for hd in range(hb):
            s = _dot_nt(q_ref[0, hd], k_ref[0, hd, pl.ds(k0, t), :])
            if diagonal:
                row = lax.broadcasted_iota(I32, s.shape, 0)
                col = lax.broadcasted_iota(I32, s.shape, 1)
                s = jnp.where(row >= col, s, -jnp.inf)
            m_prev = m_ref[hd]
            m_new = jnp.maximum(m_prev, jnp.max(s, axis=1, keepdims=True))
            p = jnp.exp(s - jnp.tile(m_new, (1, t // LANES)))
            alpha = jnp.exp(m_prev - m_new)
            l_ref[hd] = alpha * l_ref[hd] + jnp.sum(p, axis=1, keepdims=True)
            acc_ref[hd] = alpha * acc_ref[hd] + _dot(p.astype(BF16), v_ref[0, hd, pl.ds(k0, t), :])
            m_ref[hd] = m_new

    def full_tile(ki, carry):
        kv_tile(ki, False)
        return carry

    lax.fori_loop(0, qi, full_tile, 0)
    kv_tile(qi, True)
    for hd in range(hb):
        o_ref[0, :, hd * V_HEAD_DIM:(hd + 1) * V_HEAD_DIM] = (
            acc_ref[hd] / l_ref[hd]).astype(o_ref.dtype)


def _attention(q, k, v):
    b, nh, s, _ = q.shape
    t = min(ATTN_TILE, s)
    hb = ATTN_HEADS
    kv_spec = lambda w: pl.BlockSpec((1, hb, s, w), lambda bi, gi, qi: (bi, gi, 0, 0))
    stat = pltpu.VMEM((hb, t, LANES), F32)
    return pl.pallas_call(
        _attn_kernel,
        grid=(b, nh // hb, s // t),
        in_specs=[pl.BlockSpec((1, hb, t, QK_DIM), lambda bi, gi, qi: (bi, gi, qi, 0)),
                  kv_spec(QK_DIM), kv_spec(V_HEAD_DIM)],
        out_specs=pl.BlockSpec((1, t, hb * V_HEAD_DIM), lambda bi, gi, qi: (bi, qi, gi)),
        out_shape=jax.ShapeDtypeStruct((b, s, nh * V_HEAD_DIM), BF16),
        scratch_shapes=[stat, stat, pltpu.VMEM((hb, t, V_HEAD_DIM), F32)],
        compiler_params=_params(("arbitrary",) * 3),
        name="mla_attention",
    )(q, k, v)


def _oproj_ffn_kernel(h_ref, o_ref, wo_ref, g_ref, wg_ref, wu_ref, wd_ref, out_ref, *, chunk):
    h1 = h_ref[...] + _dot(o_ref[...], wo_ref[...])
    hn = _rms(h1, g_ref[...]).astype(BF16)
    acc = h1
    for c0 in range(0, wg_ref.shape[1], chunk):
        gate = _dot(hn, wg_ref[:, c0:c0 + chunk])
        up = _dot(hn, wu_ref[:, c0:c0 + chunk])
        act = (gate * jax.nn.sigmoid(gate) * up).astype(BF16)
        acc = acc + _dot(act, wd_ref[c0:c0 + chunk, :])
    out_ref[...] = acc


def _ff_chunk(ff):
    for c in (896, 768, 640, 512, 384, 256, 128):
        if ff % c == 0:
            return c
    return ff


def _oproj_ffn(h, o, wo, g, wg, wu, wd):
    n, d = h.shape
    tm = min(ROW_TILE, n)
    full = lambda a: pl.BlockSpec(a.shape, lambda i: (0,) * a.ndim)
    g2 = g[None, :]
    return pl.pallas_call(
        functools.partial(_oproj_ffn_kernel, chunk=_ff_chunk(wg.shape[1])),
        grid=(n // tm,),
        in_specs=[pl.BlockSpec((tm, d), lambda i: (i, 0)),
                  pl.BlockSpec((tm, o.shape[1]), lambda i: (i, 0)),
                  full(wo), full(g2), full(wg), full(wu), full(wd)],
        out_specs=pl.BlockSpec((tm, d), lambda i: (i, 0)),
        out_shape=jax.ShapeDtypeStruct((n, d), F32),
        compiler_params=_params(("arbitrary",)),
        name="oproj_ffn",
    )(h, o, wo, g2, wg, wu, wd)


def _pool_kernel(h_ref, g_ref, w_ref, sc_ref, out_ref, pad_ref):
    s, d = h_ref.shape[1], h_ref.shape[2]
    gd = d // len(POOL_WINDOWS)
    h = h_ref[0]
    hn = _rms(h, g_ref[...])
    t1 = lax.broadcasted_iota(I32, (s, 1), 0) + 1
    pad_ref[0:POOL_HALO, :] = jnp.zeros((POOL_HALO, gd), F32)
    for gi, w in enumerate(POOL_WINDOWS):
        x = hn[:, gi * gd:(gi + 1) * gd]
        acc = x
        k = 1
        while k < w:
            pad_ref[POOL_HALO:POOL_HALO + s, :] = acc
            acc = acc + pad_ref[POOL_HALO - k:POOL_HALO - k + s, :]
            k *= 2
        cnt = jnp.minimum(t1, w).astype(F32)
        z = (acc / cnt - x).astype(BF16)
        y = _dot(z, w_ref[gi]) * sc_ref[:, gi * gd:(gi + 1) * gd]
        out_ref[0, :, gi * gd:(gi + 1) * gd] = h[:, gi * gd:(gi + 1) * gd] + y


def _pool(h3, g, pool_w, pool_scale):
    b, s, d = h3.shape
    gd = d // len(POOL_WINDOWS)
    full = lambda a: pl.BlockSpec(a.shape, lambda i: (0,) * a.ndim)
    g2, sc2 = g[None, :], pool_scale[None, :]
    return pl.pallas_call(
        _pool_kernel,
        grid=(b,),
        in_specs=[pl.BlockSpec((1, s, d), lambda i: (i, 0, 0)), full(g2), full(pool_w), full(sc2)],
        out_specs=pl.BlockSpec((1, s, d), lambda i: (i, 0, 0)),
        out_shape=jax.ShapeDtypeStruct((b, s, d), F32),
        scratch_shapes=[pltpu.VMEM((s + POOL_HALO, gd), F32)],
        compiler_params=_params(("arbitrary",)),
        name="pool_mixer",
    )(h3, g2, pool_w, sc2)


def _router_kernel(h_ref, g_ref, rt_ref, tri_ref, xt_ref, meta_ref, gate_ref, cnt_ref, carry_ref):
    t = h_ref.shape[0]

    @pl.when(pl.program_id(0) == 0)
    def _():
        carry_ref[...] = jnp.zeros(carry_ref.shape, F32)

    hn = _rms(h_ref[...], g_ref[...])
    _rows_to_tiles(xt_ref, hn)

    hi = hn.astype(BF16)
    lo = (hn - hi.astype(F32)).astype(BF16)
    rt = rt_ref[...]
    a = _dot_nt(rt, hi)
    logits = a[:N_EXPERTS] + a[N_EXPERTS:] + _dot_nt(rt[:N_EXPERTS], lo)

    eidx = lax.broadcasted_iota(I32, logits.shape, 0)
    m1 = jnp.max(logits, axis=0, keepdims=True)
    i1 = jnp.min(jnp.where(logits == m1, eidx, N_EXPERTS), axis=0, keepdims=True)
    oh1 = eidx == i1
    rest = jnp.where(oh1, -jnp.inf, logits)
    m2 = jnp.max(rest, axis=0, keepdims=True)
    i2 = jnp.min(jnp.where(rest == m2, eidx, N_EXPERTS), axis=0, keepdims=True)
    oh2 = eidx == i2
    e2 = jnp.exp(m2 - m1)
    den = 1.0 + e2
    g1 = 1.0 / den
    g2 = e2 / den

    sel = jnp.where(oh1 | oh2, 1.0, 0.0).astype(F32)
    pref = _dot(sel.astype(BF16), tri_ref[...])
    carry = carry_ref[:, 0:1]
    base = carry + pref
    r1 = jnp.sum(jnp.where(oh1, base, 0.0), axis=0, keepdims=True)
    r2 = jnp.sum(jnp.where(oh2, base, 0.0), axis=0, keepdims=True)
    total = carry + jnp.sum(sel, axis=1, keepdims=True)
    carry_ref[...] = jnp.broadcast_to(total, carry_ref.shape)
    cnt_ref[...] = jnp.broadcast_to(total, cnt_ref.shape).astype(I32)

    zi = jnp.zeros((SUBLANES - 4, t), I32)
    meta_ref[...] = jnp.concatenate([i1, i2, r1.astype(I32), r2.astype(I32), zi], axis=0)
    zf = jnp.zeros((SUBLANES - 2, t), F32)
    gate_ref[...] = jnp.concatenate([g1, g2, zf], axis=0)


def _router(h, g, router):
    n, d = h.shape
    t = min(ROUTER_TILE, n)
    r_t = router.T
    r_hi = r_t.astype(BF16)
    r_lo = (r_t - r_hi.astype(F32)).astype(BF16)
    rt = jnp.concatenate([r_hi, r_lo], axis=0)
    tri = jnp.triu(jnp.ones((t, t), BF16), k=1)
    full = lambda a: pl.BlockSpec(a.shape, lambda i: (0,) * a.ndim)
    g2 = g[None, :]
    return pl.pallas_call(
        _router_kernel,
        grid=(n // t,),
        in_specs=[pl.BlockSpec((t, d), lambda i: (i, 0)), full(g2), full(rt), full(tri)],
        out_specs=[pl.BlockSpec((t * SUBLANES, LANES), lambda i: (i, 0)),
                   pl.BlockSpec((SUBLANES, t), lambda i: (0, i)),
                   pl.BlockSpec((SUBLANES, t), lambda i: (0, i)),
                   pl.BlockSpec((N_EXPERTS, LANES), lambda i: (0, 0))],
        out_shape=[jax.ShapeDtypeStruct((n * SUBLANES, LANES), F32),
                   jax.ShapeDtypeStruct((SUBLANES, n), I32),
                   jax.ShapeDtypeStruct((SUBLANES, n), F32),
                   jax.ShapeDtypeStruct((N_EXPERTS, LANES), I32)],
        scratch_shapes=[pltpu.VMEM((N_EXPERTS, LANES), F32)],
        compiler_params=_params(("arbitrary",)),
        name="moe_router",
    )(h, g2, rt, tri)


def _row(ref, r):
    return ref.at[pl.ds(pl.multiple_of(r * SUBLANES, SUBLANES), SUBLANES)]


def _dispatch_kernel(d1_ref, d2_ref, zb_ref, xt_ref, xb_ref, zero_ref, sem, zsem, *, td, n_zero):
    i = pl.program_id(0)

    def wait_rows(n_rows, s):
        span = xb_ref.at[pl.ds(0, n_rows * SUBLANES)]
        pltpu.make_async_copy(span, span, s).wait()

    @pl.when(i == 0)
    def _():
        zero_ref[...] = jnp.zeros(zero_ref.shape, F32)
        for e in range(N_EXPERTS):
            def zbody(r, c):
                pltpu.make_async_copy(zero_ref, _row(xb_ref, r), zsem).start()
                return c
            lax.fori_loop(zb_ref[0, e], zb_ref[1, e], zbody, 0)

    def body(j, c):
        src = _row(xt_ref, j)
        pltpu.make_async_copy(src, _row(xb_ref, d1_ref[0, 0, j]), sem).start(priority=0)
        pltpu.make_async_copy(src, _row(xb_ref, d2_ref[0, 0, j]), sem).start(priority=1)
        return c
    lax.fori_loop(0, td, body, 0)
    wait_rows(2 * td, sem)

    @pl.when(i == 0)
    def _():
        wait_rows(n_zero, zsem)


def _dispatch(xt, d1, d2, zero_bounds, n_rows_out):
    n = xt.shape[0] // SUBLANES
    td = min(DISPATCH_TILE, n)
    steps = n // td
    n_zero = n_rows_out - 2 * n
    smem_tile = pl.BlockSpec((1, 1, td), lambda i: (i, 0, 0), memory_space=pltpu.SMEM)
    return pl.pallas_call(
        functools.partial(_dispatch_kernel, td=td, n_zero=n_zero),
        grid=(steps,),
        in_specs=[smem_tile, smem_tile,
                  pl.BlockSpec(memory_space=pltpu.SMEM),
                  pl.BlockSpec((td * SUBLANES, LANES), lambda i: (i, 0))],
        out_specs=pl.BlockSpec(memory_space=pl.ANY),
        out_shape=jax.ShapeDtypeStruct((n_rows_out * SUBLANES, LANES), F32),
        scratch_shapes=[pltpu.VMEM((SUBLANES, LANES), F32), pltpu.SemaphoreType.DMA,
                        pltpu.SemaphoreType.DMA],
        compiler_params=_params(("arbitrary",)),
        name="moe_dispatch",
    )(d1.reshape(steps, 1, td), d2.reshape(steps, 1, td), zero_bounds, xt)


def _expert_kernel(be_ref, x_ref, wg_ref, wu_ref, wd_ref, y_ref):
    del be_ref
    bm = x_ref.shape[0] // SUBLANES
    x = _tiles_to_rows(x_ref, bm).astype(BF16)
    gate = _dot(x, wg_ref[0])
    up = _dot(x, wu_ref[0])
    act = (gate * jax.nn.sigmoid(gate) * up).astype(BF16)
    _rows_to_tiles(y_ref, _dot(act, wd_ref[0]))


def _experts(xb, block_e, wg, wu, wd):
    bm = MOE_ROWS
    n_blocks = xb.shape[0] // (bm * SUBLANES)
    d, ff = wg.shape[1], wg.shape[2]
    return pl.pallas_call(
        _expert_kernel,
        grid_spec=pltpu.PrefetchScalarGridSpec(
            num_scalar_prefetch=1,
            grid=(n_blocks,),
            in_specs=[pl.BlockSpec((bm * SUBLANES, LANES), lambda i, be: (i, 0)),
                      pl.BlockSpec((1, d, ff), lambda i, be: (be[i], 0, 0)),
                      pl.BlockSpec((1, d, ff), lambda i, be: (be[i], 0, 0)),
                      pl.BlockSpec((1, ff, d), lambda i, be: (be[i], 0, 0))],
            out_specs=pl.BlockSpec((bm * SUBLANES, LANES), lambda i, be: (i, 0)),
        ),
        out_shape=jax.ShapeDtypeStruct(xb.shape, F32),
        compiler_params=_params(("arbitrary",)),
        name="moe_experts",
    )(block_e, xb, wg, wu, wd)


def _combine_kernel(d1_ref, d2_ref, h_ref, gc_ref, fn_ref, yb_ref, out_ref, b1_ref, b2_ref, sem,
                    *, final_norm):
    t = h_ref.shape[0]

    def body(j, c):
        pltpu.make_async_copy(_row(yb_ref, d1_ref[0, 0, j]), _row(b1_ref, j), sem).start(priority=0)
        pltpu.make_async_copy(_row(yb_ref, d2_ref[0, 0, j]), _row(b2_ref, j), sem).start(priority=1)
        return c
    lax.fori_loop(0, t, body, 0)
    pltpu.make_async_copy(yb_ref.at[pl.ds(0, t * SUBLANES)], b1_ref, sem).wait()
    pltpu.make_async_copy(yb_ref.at[pl.ds(0, t * SUBLANES)], b2_ref, sem).wait()

    gc = gc_ref[...]
    out = (h_ref[...] + gc[:, 0:1] * _tiles_to_rows(b1_ref, t)
           + gc[:, 1:2] * _tiles_to_rows(b2_ref, t))
    if final_norm:
        out = _rms(out, fn_ref[...])
    out_ref[...] = out


def _combine(h, yb, d1, d2, gcol, fnorm, final_norm):
    n, d = h.shape
    t = min(DMA_TILE, n)
    steps = n // t
    smem_tile = pl.BlockSpec((1, 1, t), lambda i: (i, 0, 0), memory_space=pltpu.SMEM)
    fn2 = fnorm[None, :]
    return pl.pallas_call(
        functools.partial(_combine_kernel, final_norm=final_norm),
        grid=(steps,),
        in_specs=[smem_tile, smem_tile,
                  pl.BlockSpec((t, d), lambda i: (i, 0)),
                  pl.BlockSpec((t, gcol.shape[1]), lambda i: (i, 0)),
                  pl.BlockSpec(fn2.shape, lambda i: (0, 0)),
                  pl.BlockSpec(memory_space=pl.ANY)],
        out_specs=pl.BlockSpec((t, d), lambda i: (i, 0)),
        out_shape=jax.ShapeDtypeStruct((n, d), F32),
        scratch_shapes=[pltpu.VMEM((t * SUBLANES, LANES), F32),
                        pltpu.VMEM((t * SUBLANES, LANES), F32),
                        pltpu.SemaphoreType.DMA],
        compiler_params=_params(("arbitrary",)),
        name="moe_combine",
    )(d1.reshape(steps, 1, t), d2.reshape(steps, 1, t), h, gcol, fn2, yb)


def _moe(h, g, router, wg, wu, wd, fnorm, final_norm):
    n = h.shape[0]
    bm = MOE_ROWS
    xt, meta, gates, cnt = _router(h, g, router)
    counts = cnt[:, 0]
    padded = (counts + bm - 1) // bm * bm
    pad_end = jnp.cumsum(padded)
    pad_start = pad_end - padded
    d1 = pad_start[meta[0]] + meta[2]
    d2 = pad_start[meta[1]] + meta[3]
    n_blocks = (n * 2) // bm + N_EXPERTS
    n_rows = n_blocks * bm
    zero_lo = pad_start + counts
    zero_hi = jnp.concatenate([pad_start[1:], jnp.array([n_rows], I32)])
    zero_bounds = jnp.stack([zero_lo, zero_hi]).astype(I32)
    block_e = jnp.clip(jnp.searchsorted(pad_end, jnp.arange(n_blocks, dtype=I32) * bm, side="right"),
                       0, N_EXPERTS - 1).astype(I32)
    xb = _dispatch(xt, d1, d2, zero_bounds, n_rows)
    yb = _experts(xb, block_e, wg, wu, wd)
    gcol = gates[:2].T
    return _combine(h, yb, d1, d2, gcol, fnorm, final_norm)


def _pad_ff(w, axis):
    ff = w.shape[axis]
    pad = (-ff) % FF_LANE_PAD
    if pad == 0:
        return w
    widths = [(0, 0)] * w.ndim
    widths[axis] = (0, pad)
    return jnp.pad(w, widths)


def kernel(x, positions, attn_norm, ffn_norm, mla_w_down, mla_q_norm, mla_w_uq, mla_kv_norm,
           mla_w_ukv, mla_w_o, pool_w, pool_scale, ffn_w_gate, ffn_w_up, ffn_w_down,
           moe_router, moe_w_gate, moe_w_up, moe_w_down, final_norm):
    b, s, d = x.shape
    n = b * s
    depth = attn_norm.shape[0]
    h = x.reshape(n, d)
    cs = _rope_table(positions)
    for layer in range(depth):
        a = layer // 2
        last = layer == depth - 1
        if layer % 2 == 0:
            wd, wuq, wukv = _prep_mla_weights(mla_w_down[a], mla_w_uq[a], mla_w_ukv[a])
            q, k, v = _mla_proj(h, attn_norm[layer], wd, mla_q_norm[a], wuq, mla_kv_norm[a],
                                wukv, cs, b, s)
            o = _attention(q, k, v).reshape(n, N_HEADS * V_HEAD_DIM)
            h = _oproj_ffn(h, o, mla_w_o[a].astype(BF16), ffn_norm[layer],
                           ffn_w_gate[a].astype(BF16), ffn_w_up[a].astype(BF16),
                           ffn_w_down[a].astype(BF16))
            if last:
                h = _final_norm(h, final_norm)
        else:
            h = _pool(h.reshape(b, s, d), attn_norm[layer], pool_w[a].astype(BF16),
                      pool_scale[a]).reshape(n, d)
            h = _moe(h, ffn_norm[layer], moe_router[a],
                     _pad_ff(moe_w_gate[a], 2).astype(BF16), _pad_ff(moe_w_up[a], 2).astype(BF16),
                     _pad_ff(moe_w_down[a], 1).astype(BF16), final_norm, last)
    return h.reshape(b, s, d)


def _final_norm_kernel(h_ref, g_ref, o_ref):
    o_ref[...] = _rms(h_ref[...], g_ref[...])


def _final_norm(h, g):
    n, d = h.shape
    tm = min(ROW_TILE, n)
    g2 = g[None, :]
    return pl.pallas_call(
        _final_norm_kernel,
        grid=(n // tm,),
        in_specs=[pl.BlockSpec((tm, d), lambda i: (i, 0)), pl.BlockSpec(g2.shape, lambda i: (0, 0))],
        out_specs=pl.BlockSpec((tm, d), lambda i: (i, 0)),
        out_shape=jax.ShapeDtypeStruct((n, d), F32),
        compiler_params=_params(("arbitrary",)),
        name="final_norm",
    )(h, g2)
```

```python
import functools

import jax
import jax.numpy as jnp
from jax import lax
from jax.experimental import pallas as pl
from jax.experimental.pallas import tpu as pltpu

F32 = jnp.float32
BF16 = jnp.bfloat16
I32 = jnp.int32

N_HEADS = 8
QK_NOPE_DIM = 128
QK_ROPE_DIM = 64
QK_DIM = QK_NOPE_DIM + QK_ROPE_DIM
V_HEAD_DIM = 128
Q_LORA_RANK = 512
KV_LORA_RANK = 256
ROPE_THETA = 10000.0
POOL_WINDOWS = (2, 4, 8, 16)
POOL_HALO = 16
N_EXPERTS = 8
RMS_EPS = 1e-6

LANES = 128
SUBLANES = 8
VMEM_LIMIT = 56 * 1024 * 1024

ROW_TILE = 512
ATTN_TILE = 512
ATTN_HEADS = 8
FF_CHUNK = 1024
ROUTER_TILE = 512
MOE_ROWS = 512
FF_LANE_PAD = 128


def _rms(x, g):
    ms = jnp.mean(x * x, axis=-1, keepdims=True)
    return x * lax.rsqrt(ms + RMS_EPS) * g


def _dot(a, b):
    return jnp.dot(a, b, preferred_element_type=F32)


def _dot_nt(a, b):
    return lax.dot_general(a, b, (((1,), (1,)), ((), ())), preferred_element_type=F32)


def _params(sem):
    return pltpu.CompilerParams(dimension_semantics=sem, vmem_limit_bytes=VMEM_LIMIT)


def _rows_to_tiles(ref, val):
    t = val.shape[0]
    for j in range(SUBLANES):
        ref[pl.ds(j, t, stride=SUBLANES), :] = val[:, j * LANES:(j + 1) * LANES]


def _tiles_to_rows(ref, t):
    return jnp.concatenate(
        [ref[pl.ds(j, t, stride=SUBLANES), :] for j in range(SUBLANES)], axis=-1)


def _rope_kernel(pos_ref, invf_ref, cos_ref, sin_ref):
    ang = pos_ref[...].astype(F32) * invf_ref[...]
    cos_ref[...] = jnp.cos(ang)
    sin_ref[...] = jnp.sin(ang)


def _rope_table(positions):
    n = positions.size
    half = QK_ROPE_DIM // 2
    per_row = LANES // half
    inv_freq = ROPE_THETA ** (-jnp.arange(0, QK_ROPE_DIM, 2, dtype=F32) / QK_ROPE_DIM)
    pos = jnp.repeat(positions.reshape(n // per_row, per_row), half, axis=1)
    invf = jnp.tile(inv_freq, per_row)[None, :]
    rows = n // per_row
    tr = min(rows, 2048)
    cos, sin = pl.pallas_call(
        _rope_kernel,
        grid=(rows // tr,),
        in_specs=[pl.BlockSpec((tr, LANES), lambda i: (i, 0)),
                  pl.BlockSpec((1, LANES), lambda i: (0, 0))],
        out_specs=[pl.BlockSpec((tr, LANES), lambda i: (i, 0)),
                   pl.BlockSpec((tr, LANES), lambda i: (i, 0))],
        out_shape=[jax.ShapeDtypeStruct((rows, LANES), F32)] * 2,
        compiler_params=_params(("arbitrary",)),
        name="rope_table",
    )(pos, invf)
    cos = cos.reshape(n, half)
    sin = sin.reshape(n, half)
    return jnp.concatenate([cos, cos, sin, sin], axis=-1)


def _mla_proj_kernel(h_ref, g_ref, wd_ref, qn_ref, wuq_ref, kvn_ref, wukv_ref, cs_ref,
                     q_ref, k_ref, v_ref):
    hn = _rms(h_ref[...], g_ref[...]).astype(BF16)
    down = _dot(hn, wd_ref[...])
    kv0 = Q_LORA_RANK
    kr0 = Q_LORA_RANK + KV_LORA_RANK
    cq = _rms(down[:, :kv0], qn_ref[...]).astype(BF16)
    ckv = _rms(down[:, kv0:kr0], kvn_ref[...]).astype(BF16)
    cs = cs_ref[...]
    cos2 = cs[:, :QK_ROPE_DIM]
    sin2 = cs[:, QK_ROPE_DIM:]
    k_rope = (down[:, kr0:kr0 + QK_ROPE_DIM] * cos2
              + down[:, kr0 + QK_ROPE_DIM:kr0 + 2 * QK_ROPE_DIM] * sin2).astype(BF16)
    q = _dot(cq, wuq_ref[...])
    kv = _dot(ckv, wukv_ref[...])
    scale = QK_DIM ** -0.5
    rope0 = N_HEADS * QK_NOPE_DIM
    rot0 = rope0 + N_HEADS * QK_ROPE_DIM
    v0 = N_HEADS * QK_NOPE_DIM
    for hd in range(N_HEADS):
        q_nope = q[:, hd * QK_NOPE_DIM:(hd + 1) * QK_NOPE_DIM] * scale
        q_rope = (q[:, rope0 + hd * QK_ROPE_DIM:rope0 + (hd + 1) * QK_ROPE_DIM] * cos2
                  + q[:, rot0 + hd * QK_ROPE_DIM:rot0 + (hd + 1) * QK_ROPE_DIM] * sin2) * scale
        q_ref[0, hd, :, 0:QK_NOPE_DIM] = q_nope.astype(BF16)
        q_ref[0, hd, :, QK_NOPE_DIM:QK_DIM] = q_rope.astype(BF16)
        k_ref[0, hd, :, 0:QK_NOPE_DIM] = kv[:, hd * QK_NOPE_DIM:(hd + 1) * QK_NOPE_DIM].astype(BF16)
        k_ref[0, hd, :, QK_NOPE_DIM:QK_DIM] = k_rope
        v_ref[0, hd] = kv[:, v0 + hd * V_HEAD_DIM:v0 + (hd + 1) * V_HEAD_DIM].astype(BF16)


def _half_rotated(w):
    half = w.shape[-1] // 2
    return jnp.concatenate([-w[..., half:], w[..., :half]], axis=-1)


def _prep_mla_weights(w_down, w_uq, w_ukv):
    d = w_down.shape[0]
    kr0 = Q_LORA_RANK + KV_LORA_RANK
    wd = jnp.concatenate([w_down, _half_rotated(w_down[:, kr0:])], axis=1).astype(BF16)
    wq = w_uq.reshape(Q_LORA_RANK, N_HEADS, QK_DIM)
    wq_nope = wq[:, :, :QK_NOPE_DIM].reshape(Q_LORA_RANK, -1)
    wq_rope = wq[:, :, QK_NOPE_DIM:]
    wuq = jnp.concatenate([wq_nope, wq_rope.reshape(Q_LORA_RANK, -1),
                           _half_rotated(wq_rope).reshape(Q_LORA_RANK, -1)], axis=1).astype(BF16)
    wkv = w_ukv.reshape(KV_LORA_RANK, N_HEADS, QK_NOPE_DIM + V_HEAD_DIM)
    wukv = jnp.concatenate([wkv[:, :, :QK_NOPE_DIM].reshape(KV_LORA_RANK, -1),
                            wkv[:, :, QK_NOPE_DIM:].reshape(KV_LORA_RANK, -1)], axis=1).astype(BF16)
    del d
    return wd, wuq, wukv


def _mla_proj(h, g, wd, q_norm, wuq, kv_norm, wukv, cs, batch, seq):
    n, d = h.shape
    tm = min(ROW_TILE, seq)
    per_b = seq // tm
    full = lambda a: pl.BlockSpec(a.shape, lambda i: (0,) * a.ndim)
    head_spec = lambda w: pl.BlockSpec((1, N_HEADS, tm, w), lambda i: (i // per_b, 0, i % per_b, 0))
    g2, qn2, kvn2 = g[None, :], q_norm[None, :], kv_norm[None, :]
    return pl.pallas_call(
        _mla_proj_kernel,
        grid=(n // tm,),
        in_specs=[pl.BlockSpec((tm, d), lambda i: (i, 0)), full(g2), full(wd), full(qn2),
                  full(wuq), full(kvn2), full(wukv),
                  pl.BlockSpec((tm, LANES), lambda i: (i, 0))],
        out_specs=[head_spec(QK_DIM), head_spec(QK_DIM), head_spec(V_HEAD_DIM)],
        out_shape=[jax.ShapeDtypeStruct((batch, N_HEADS, seq, QK_DIM), BF16),
                   jax.ShapeDtypeStruct((batch, N_HEADS, seq, QK_DIM), BF16),
                   jax.ShapeDtypeStruct((batch, N_HEADS, seq, V_HEAD_DIM), BF16)],
        compiler_params=_params(("arbitrary",)),
        name="mla_proj",
    )(h, g2, wd, qn2, wuq, kvn2, wukv, cs)


def _attn_kernel(q_ref, k_ref, v_ref, o_ref, m_ref, l_ref, acc_ref):
    hb, t = q_ref.shape[1], q_ref.shape[2]
    qi = pl.program_id(2)
    m_ref[...] = jnp.full(m_ref.shape, -jnp.inf, F32)
    l_ref[...] = jnp.zeros(l_ref.shape, F32)
    acc_ref[...] = jnp.zeros(acc_ref.shape, F32)

    def kv_tile(ki, diagonal):
        k0 = pl.multiple_of(ki * t, t)
        for hd in range(hb):
            s = _dot_nt(q_ref[0, hd], k_ref[0, hd, pl.ds(k0, t), :])
            if diagonal:
                row = lax.broadcasted_iota(I32, s.shape, 0)
                col = lax.broadcasted_iota(I32, s.shape, 1)
                s = jnp.where(row >= col, s, -jnp.inf)
            m_prev = m_ref[hd]
            m_new = jnp.maximum(m_prev, jnp.max(s, axis=1, keepdims=True))
            p = jnp.exp(s - jnp.tile(m_new, (1, t // LANES)))
            alpha = jnp.exp(m_prev - m_new)
            l_ref[hd] = alpha * l_ref[hd] + jnp.sum(p, axis=1, keepdims=True)
            acc_ref[hd] = alpha * acc_ref[hd] + _dot(p.astype(BF16), v_ref[0, hd, pl.ds(k0, t), :])
            m_ref[hd] = m_new

    def full_tile(ki, carry):
        kv_tile(ki, False)
        return carry

    lax.fori_loop(0, qi, full_tile, 0)
    kv_tile(qi, True)
    for hd in range(hb):
        o_ref[0, :, hd * V_HEAD_DIM:(hd + 1) * V_HEAD_DIM] = (
            acc_ref[hd] / l_ref[hd]).astype(o_ref.dtype)


def _attention(q, k, v):
    b, nh, s, _ = q.shape
    t = min(ATTN_TILE, s)
    hb = ATTN_HEADS
    kv_spec = lambda w: pl.BlockSpec((1, hb, s, w), lambda bi, gi, qi: (bi, gi, 0, 0))
    stat = pltpu.VMEM((hb, t, LANES), F32)
    return pl.pallas_call(
        _attn_kernel,
        grid=(b, nh // hb, s // t),
        in_specs=[pl.BlockSpec((1, hb, t, QK_DIM), lambda bi, gi, qi: (bi, gi, qi, 0)),
                  kv_spec(QK_DIM), kv_spec(V_HEAD_DIM)],
        out_specs=pl.BlockSpec((1, t, hb * V_HEAD_DIM), lambda bi, gi, qi: (bi, qi, gi)),
        out_shape=jax.ShapeDtypeStruct((b, s, nh * V_HEAD_DIM), BF16),
        scratch_shapes=[stat, stat, pltpu.VMEM((hb, t, V_HEAD_DIM), F32)],
        compiler_params=_params(("arbitrary",) * 3),
        name="mla_attention",
    )(q, k, v)


def _oproj_ffn_kernel(h_ref, o_ref, wo_ref, g_ref, wg_ref, wu_ref, wd_ref, out_ref, *, chunk):
    h1 = h_ref[...] + _dot(o_ref[...], wo_ref[...])
    hn = _rms(h1, g_ref[...]).astype(BF16)
    acc = h1
    ff = wg_ref.shape[1]
    for c0 in range(0, ff, chunk):
        c1 = min(c0 + chunk, ff)
        gate = _dot(hn, wg_ref[:, c0:c1])
        up = _dot(hn, wu_ref[:, c0:c1])
        act = (gate * jax.nn.sigmoid(gate) * up).astype(BF16)
        acc = acc + _dot(act, wd_ref[c0:c1, :])
    out_ref[...] = acc


def _oproj_ffn(h, o, wo, g, wg, wu, wd):
    n, d = h.shape
    tm = min(ROW_TILE, n)
    full = lambda a: pl.BlockSpec(a.shape, lambda i: (0,) * a.ndim)
    g2 = g[None, :]
    return pl.pallas_call(
        functools.partial(_oproj_ffn_kernel, chunk=FF_CHUNK),
        grid=(n // tm,),
        in_specs=[pl.BlockSpec((tm, d), lambda i: (i, 0)),
                  pl.BlockSpec((tm, o.shape[1]), lambda i: (i, 0)),
                  full(wo), full(g2), full(wg), full(wu), full(wd)],
        out_specs=pl.BlockSpec((tm, d), lambda i: (i, 0)),
        out_shape=jax.ShapeDtypeStruct((n, d), F32),
        compiler_params=_params(("arbitrary",)),
        name="oproj_ffn",
    )(h, o, wo, g2, wg, wu, wd)


def _pool_kernel(h_ref, g_ref, w_ref, sc_ref, out_ref, pad_ref):
    s, d = h_ref.shape[1], h_ref.shape[2]
    gd = d // len(POOL_WINDOWS)
    h = h_ref[0]
    hn = _rms(h, g_ref[...])
    t1 = lax.broadcasted_iota(I32, (s, 1), 0) + 1
    pad_ref[0:POOL_HALO, :] = jnp.zeros((POOL_HALO, gd), F32)
    for gi, w in enumerate(POOL_WINDOWS):
        x = hn[:, gi * gd:(gi + 1) * gd]
        acc = x
        k = 1
        while k < w:
            pad_ref[POOL_HALO:POOL_HALO + s, :] = acc
            acc = acc + pad_ref[POOL_HALO - k:POOL_HALO - k + s, :]
            k *= 2
        cnt = jnp.minimum(t1, w).astype(F32)
        z = (acc / cnt - x).astype(BF16)
        y = _dot(z, w_ref[gi]) * sc_ref[:, gi * gd:(gi + 1) * gd]
        out_ref[0, :, gi * gd:(gi + 1) * gd] = h[:, gi * gd:(gi + 1) * gd] + y


def _pool(h3, g, pool_w, pool_scale):
    b, s, d = h3.shape
    gd = d // len(POOL_WINDOWS)
    full = lambda a: pl.BlockSpec(a.shape, lambda i: (0,) * a.ndim)
    g2, sc2 = g[None, :], pool_scale[None, :]
    return pl.pallas_call(
        _pool_kernel,
        grid=(b,),
        in_specs=[pl.BlockSpec((1, s, d), lambda i: (i, 0, 0)), full(g2), full(pool_w), full(sc2)],
        out_specs=pl.BlockSpec((1, s, d), lambda i: (i, 0, 0)),
        out_shape=jax.ShapeDtypeStruct((b, s, d), F32),
        scratch_shapes=[pltpu.VMEM((s + POOL_HALO, gd), F32)],
        compiler_params=_params(("arbitrary",)),
        name="pool_mixer",
    )(h3, g2, pool_w, sc2)


def _row(ref, r):
    return ref.at[pl.ds(pl.multiple_of(r * SUBLANES, SUBLANES), SUBLANES)]


def _rows(ref, r, n_rows):
    return ref.at[pl.ds(pl.multiple_of(r * SUBLANES, SUBLANES), n_rows * SUBLANES)]


def _segment_copies(n_rows, start_copy, max_rows):
    for b in range(max_rows.bit_length()):
        size = 1 << b

        @pl.when(((n_rows >> b) & 1) == 1)
        def _():
            start_copy(n_rows & (size - 1), size)


SEG_STRIDE = 2 * N_EXPERTS


def _router_kernel(h_ref, g_ref, rt_ref, tri_ref, meta_ref, gate_ref, seg_ref, carry_ref):
    t = h_ref.shape[0]

    @pl.when(pl.program_id(0) == 0)
    def _():
        carry_ref[...] = jnp.zeros(carry_ref.shape, F32)

    hn = _rms(h_ref[...], g_ref[...])

    hi = hn.astype(BF16)
    lo = (hn - hi.astype(F32)).astype(BF16)
    rt = rt_ref[...]
    a = _dot_nt(rt, hi)
    logits = a[:N_EXPERTS] + a[N_EXPERTS:] + _dot_nt(rt[:N_EXPERTS], lo)

    eidx = lax.broadcasted_iota(I32, logits.shape, 0)
    m1 = jnp.max(logits, axis=0, keepdims=True)
    i1 = jnp.min(jnp.where(logits == m1, eidx, N_EXPERTS), axis=0, keepdims=True)
    oh1 = eidx == i1
    rest = jnp.where(oh1, -jnp.inf, logits)
    m2 = jnp.max(rest, axis=0, keepdims=True)
    i2 = jnp.min(jnp.where(rest == m2, eidx, N_EXPERTS), axis=0, keepdims=True)
    oh2 = eidx == i2
    e2 = jnp.exp(m2 - m1)
    den = 1.0 + e2
    g1 = 1.0 / den
    g2 = e2 / den

    sel = jnp.where(oh1 | oh2, 1.0, 0.0).astype(F32)
    pref = _dot(sel.astype(BF16), tri_ref[...])
    carry = carry_ref[...]
    cnt = jnp.broadcast_to(jnp.sum(sel, axis=1, keepdims=True), (N_EXPERTS, LANES))
    erow = lax.broadcasted_iota(I32, cnt.shape, 0)
    incl = cnt
    for sh in (1, 2, 4):
        incl = incl + jnp.where(erow >= sh, pltpu.roll(incl, sh, axis=0), 0.0)
    lrow = (incl - cnt)[:, 0:1] + pref
    ls1 = jnp.sum(jnp.where(oh1, lrow, 0.0), axis=0, keepdims=True).astype(I32)
    ls2 = jnp.sum(jnp.where(oh2, lrow, 0.0), axis=0, keepdims=True).astype(I32)
    carry_ref[...] = carry + cnt

    zi = jnp.zeros((SUBLANES - 4, t), I32)
    meta_ref[...] = jnp.concatenate([i1, i2, ls1, ls2, zi], axis=0)
    zf = jnp.zeros((SUBLANES - 2, t), F32)
    gate_ref[...] = jnp.concatenate([g1, g2, zf], axis=0)
    seg_ref[0] = jnp.concatenate([cnt, carry], axis=0).astype(I32)


def _router(h, g, router):
    n, d = h.shape
    t = min(ROUTER_TILE, n)
    r_t = router.T
    r_hi = r_t.astype(BF16)
    r_lo = (r_t - r_hi.astype(F32)).astype(BF16)
    rt = jnp.concatenate([r_hi, r_lo], axis=0)
    tri = jnp.triu(jnp.ones((t, t), BF16), k=1)
    full = lambda a: pl.BlockSpec(a.shape, lambda i: (0,) * a.ndim)
    g2 = g[None, :]
    return pl.pallas_call(
        _router_kernel,
        grid=(n // t,),
        in_specs=[pl.BlockSpec((t, d), lambda i: (i, 0)), full(g2), full(rt), full(tri)],
        out_specs=[pl.BlockSpec((SUBLANES, t), lambda i: (0, i)),
                   pl.BlockSpec((SUBLANES, t), lambda i: (0, i)),
                   pl.BlockSpec((1, SEG_STRIDE, LANES), lambda i: (i, 0, 0))],
        out_shape=[jax.ShapeDtypeStruct((SUBLANES, n), I32),
                   jax.ShapeDtypeStruct((SUBLANES, n), F32),
                   jax.ShapeDtypeStruct((n // t, SEG_STRIDE, LANES), I32)],
        scratch_shapes=[pltpu.VMEM((N_EXPERTS, LANES), F32)],
        compiler_params=_params(("arbitrary",)),
        name="moe_router",
    )(h, g2, rt, tri)


def _dispatch_kernel(seg_ref, zb_ref, h_ref, g_ref, mrow_ref, xb_ref, buf_ref, zero_ref, sem, zsem,
                     *, n_zero):
    t = h_ref.shape[0]
    i = pl.program_id(0)
    steps = pl.num_programs(0)
    slot = lax.rem(i, 2)

    def wait_rows(n_rows, s):
        span = _rows(xb_ref, 0, n_rows)
        pltpu.make_async_copy(span, span, s).wait()

    @pl.when(i == 0)
    def _():
        zero_ref[...] = jnp.zeros(zero_ref.shape, F32)
        for e in range(N_EXPERTS):
            def zstart(r, c):
                pltpu.make_async_copy(zero_ref, _row(xb_ref, r), zsem).start()
                return c
            lax.fori_loop(zb_ref[e], zb_ref[N_EXPERTS + e], zstart, 0)

    hi = _rms(h_ref[...], g_ref[...]).astype(BF16)
    srow = lax.broadcasted_iota(I32, (2 * t, t), 0)
    perm = jnp.where((srow == mrow_ref[2:3, :]) | (srow == mrow_ref[3:4, :]), 1.0, 0.0).astype(BF16)
    xs = _dot(perm, hi)

    @pl.when(i >= 2)
    def _():
        wait_rows(2 * t, sem.at[slot])
    buf = buf_ref.at[slot]
    _rows_to_tiles(buf, xs)

    l_e = 0
    for e in range(N_EXPERTS):
        c_e = seg_ref[i * SEG_STRIDE + e]

        def start_copy(off, size, src=l_e, dst=seg_ref[i * SEG_STRIDE + N_EXPERTS + e]):
            pltpu.make_async_copy(_rows(buf, src + off, size), _rows(xb_ref, dst + off, size),
                                  sem.at[slot]).start()
        _segment_copies(c_e, start_copy, t)
        l_e = l_e + c_e

    @pl.when(i == steps - 1)
    def _():
        wait_rows(2 * t, sem.at[slot])

        @pl.when(steps > 1)
        def _():
            wait_rows(2 * t, sem.at[1 - slot])
        wait_rows(n_zero, zsem)


def _dispatch(h, g, meta, seg, zero_bounds, n_rows):
    n, d = h.shape
    t = min(ROUTER_TILE, n)
    g2 = g[None, :]
    return pl.pallas_call(
        functools.partial(_dispatch_kernel, n_zero=n_rows - 2 * n),
        grid_spec=pltpu.PrefetchScalarGridSpec(
            num_scalar_prefetch=2,
            grid=(n // t,),
            in_specs=[pl.BlockSpec((t, d), lambda i, sg, zb: (i, 0)),
                      pl.BlockSpec(g2.shape, lambda i, sg, zb: (0, 0)),
                      pl.BlockSpec((SUBLANES, t), lambda i, sg, zb: (0, i))],
            out_specs=pl.BlockSpec(memory_space=pl.ANY),
            scratch_shapes=[pltpu.VMEM((2, 2 * t * SUBLANES, LANES), F32),
                            pltpu.VMEM((SUBLANES, LANES), F32),
                            pltpu.SemaphoreType.DMA((2,)),
                            pltpu.SemaphoreType.DMA],
        ),
        out_shape=jax.ShapeDtypeStruct((n_rows * SUBLANES, LANES), F32),
        compiler_params=_params(("arbitrary",)),
        name="moe_dispatch",
    )(seg, zero_bounds, h, g2, meta)


def _expert_kernel(be_ref, x_ref, wgu_ref, wd_ref, y_ref):
    del be_ref
    bm = x_ref.shape[0] // SUBLANES
    x = _tiles_to_rows(x_ref, bm).astype(BF16)
    gu = _dot(x, wgu_ref[0])
    acts = []
    for c0 in range(0, gu.shape[1], 2 * LANES):
        gate = gu[:, c0:c0 + LANES]
        up = gu[:, c0 + LANES:c0 + 2 * LANES]
        acts.append((gate * jax.nn.sigmoid(gate) * up).astype(BF16))
    act = jnp.concatenate(acts, axis=-1)
    _rows_to_tiles(y_ref, _dot(act, wd_ref[0]))


def _interleave_gate_up(wg, wu):
    e, d, ff = wg.shape
    g = wg.reshape(e, d, ff // LANES, 1, LANES)
    u = wu.reshape(e, d, ff // LANES, 1, LANES)
    return jnp.concatenate([g, u], axis=3).reshape(e, d, 2 * ff)


def _experts(xb, block_e, wgu, wd, bm):
    d, ff = wd.shape[2], wd.shape[1]
    return pl.pallas_call(
        _expert_kernel,
        grid_spec=pltpu.PrefetchScalarGridSpec(
            num_scalar_prefetch=1,
            grid=(block_e.shape[0],),
            in_specs=[pl.BlockSpec((bm * SUBLANES, LANES), lambda i, be: (i, 0)),
                      pl.BlockSpec((1, d, 2 * ff), lambda i, be: (be[i], 0, 0)),
                      pl.BlockSpec((1, ff, d), lambda i, be: (be[i], 0, 0))],
            out_specs=pl.BlockSpec((bm * SUBLANES, LANES), lambda i, be: (i, 0)),
        ),
        out_shape=jax.ShapeDtypeStruct(xb.shape, F32),
        compiler_params=_params(("arbitrary",)),
        name="moe_experts",
    )(block_e, xb, wgu, wd)


def _combine_kernel(seg_ref, h_ref, mrow_ref, grow_ref, mcol_ref, fn_ref, yb_ref, out_ref,
                    buf_ref, sem, *, final_norm):
    t = h_ref.shape[0]
    i = pl.program_id(0)
    steps = pl.num_programs(0)
    slot = lax.rem(i, 2)

    def fetch(tile, sl):
        buf = buf_ref.at[sl]
        l_e = 0
        for e in range(N_EXPERTS):
            c_e = seg_ref[tile * SEG_STRIDE + e]

            def start_copy(off, size, src=seg_ref[tile * SEG_STRIDE + N_EXPERTS + e], dst=l_e):
                pltpu.make_async_copy(_rows(yb_ref, src + off, size), _rows(buf, dst + off, size),
                                      sem.at[sl]).start()
            _segment_copies(c_e, start_copy, t)
            l_e = l_e + c_e

    @pl.when(i == 0)
    def _():
        fetch(0, 0)

    @pl.when(i + 1 < steps)
    def _():
        fetch(i + 1, 1 - slot)

    buf = buf_ref.at[slot]
    pltpu.make_async_copy(_rows(yb_ref, 0, 2 * t), buf, sem.at[slot]).wait()
    ys = _tiles_to_rows(buf, 2 * t)

    ls1 = mrow_ref[2:3, :]
    ls2 = mrow_ref[3:4, :]
    srow = lax.broadcasted_iota(I32, (2 * t, t), 0)
    w = (jnp.where(srow == ls1, grow_ref[0:1, :], 0.0)
         + jnp.where(srow == ls2, grow_ref[1:2, :], 0.0))
    ysg = ys * jnp.sum(w, axis=1, keepdims=True)
    hi = ysg.astype(BF16)
    lo = (ysg - hi.astype(F32)).astype(BF16)
    scol = lax.broadcasted_iota(I32, (t, 2 * t), 1)
    unsort = jnp.where((scol == mcol_ref[:, 0:1]) | (scol == mcol_ref[:, 1:2]), 1.0, 0.0).astype(BF16)
    out = h_ref[...] + _dot(unsort, hi) + _dot(unsort, lo)
    if final_norm:
        out = _rms(out, fn_ref[...])
    out_ref[...] = out


def _combine(h, yb, seg, meta, gates, fnorm, final_norm):
    n, d = h.shape
    t = min(ROUTER_TILE, n)
    mcol = meta[2:4].T
    fn2 = fnorm[None, :]
    return pl.pallas_call(
        functools.partial(_combine_kernel, final_norm=final_norm),
        grid_spec=pltpu.PrefetchScalarGridSpec(
            num_scalar_prefetch=1,
            grid=(n // t,),
            in_specs=[pl.BlockSpec((t, d), lambda i, sg: (i, 0)),
                      pl.BlockSpec((SUBLANES, t), lambda i, sg: (0, i)),
                      pl.BlockSpec((SUBLANES, t), lambda i, sg: (0, i)),
                      pl.BlockSpec((t, 2), lambda i, sg: (i, 0)),
                      pl.BlockSpec(fn2.shape, lambda i, sg: (0, 0)),
                      pl.BlockSpec(memory_space=pl.ANY)],
            out_specs=pl.BlockSpec((t, d), lambda i, sg: (i, 0)),
            scratch_shapes=[pltpu.VMEM((2, 2 * t * SUBLANES, LANES), F32),
                            pltpu.SemaphoreType.DMA((2,))],
        ),
        out_shape=jax.ShapeDtypeStruct((n, d), F32),
        compiler_params=_params(("arbitrary",)),
        name="moe_combine",
    )(seg, h, meta, gates, mcol, fn2, yb)


def _moe(h, g, router, wgu, wd, fnorm, final_norm):
    n = h.shape[0]
    bm = min(MOE_ROWS, n)
    meta, gates, segv = _router(h, g, router)
    tile_cnt = segv[:, :N_EXPERTS, 0]
    tile_base = segv[:, N_EXPERTS:, 0]
    counts = tile_cnt[-1] + tile_base[-1]
    padded = (counts + bm - 1) // bm * bm
    pad_end = jnp.cumsum(padded)
    pad_start = pad_end - padded
    n_blocks = (n * 2) // bm + N_EXPERTS
    n_rows = n_blocks * bm
    seg = jnp.concatenate([tile_cnt, tile_base + pad_start[None, :]], axis=1).reshape(-1).astype(I32)
    zero_bounds = jnp.concatenate([pad_start + counts, pad_start[1:],
                                   jnp.array([n_rows], I32)]).astype(I32)
    block_e = jnp.clip(jnp.searchsorted(pad_end, jnp.arange(n_blocks, dtype=I32) * bm, side="right"),
                       0, N_EXPERTS - 1).astype(I32)
    xb = _dispatch(h, g, meta, seg, zero_bounds, n_rows)
    yb = _experts(xb, block_e, wgu, wd, bm)
    return _combine(h, yb, seg, meta, gates, fnorm, final_norm)


def _pad_ff(w, axis):
    ff = w.shape[axis]
    pad = (-ff) % FF_LANE_PAD
    if pad == 0:
        return w
    widths = [(0, 0)] * w.ndim
    widths[axis] = (0, pad)
    return jnp.pad(w, widths)


def kernel(x, positions, attn_norm, ffn_norm, mla_w_down, mla_q_norm, mla_w_uq, mla_kv_norm,
           mla_w_ukv, mla_w_o, pool_w, pool_scale, ffn_w_gate, ffn_w_up, ffn_w_down,
           moe_router, moe_w_gate, moe_w_up, moe_w_down, final_norm):
    b, s, d = x.shape
    n = b * s
    depth = attn_norm.shape[0]
    h = x.reshape(n, d)
    cs = _rope_table(positions)
    for layer in range(depth):
        a = layer // 2
        last = layer == depth - 1
        if layer % 2 == 0:
            wd, wuq, wukv = _prep_mla_weights(mla_w_down[a], mla_w_uq[a], mla_w_ukv[a])
            q, k, v = _mla_proj(h, attn_norm[layer], wd, mla_q_norm[a], wuq, mla_kv_norm[a],
                                wukv, cs, b, s)
            o = _attention(q, k, v).reshape(n, N_HEADS * V_HEAD_DIM)
            h = _oproj_ffn(h, o, mla_w_o[a].astype(BF16), ffn_norm[layer],
                           ffn_w_gate[a].astype(BF16), ffn_w_up[a].astype(BF16),
                           ffn_w_down[a].astype(BF16))
            if last:
                h = _final_norm(h, final_norm)
        else:
            h = _pool(h.reshape(b, s, d), attn_norm[layer], pool_w[a].astype(BF16),
                      pool_scale[a]).reshape(n, d)
            wgu = _interleave_gate_up(_pad_ff(moe_w_gate[a], 2).astype(BF16),
                                      _pad_ff(moe_w_up[a], 2).astype(BF16))
            h = _moe(h, ffn_norm[layer], moe_router[a], wgu,
                     _pad_ff(moe_w_down[a], 1).astype(BF16), final_norm, last)
    return h.reshape(b, s, d)


def _final_norm_kernel(h_ref, g_ref, o_ref):
    o_ref[...] = _rms(h_ref[...], g_ref[...])


def _final_norm(h, g):
    n, d = h.shape
    tm = min(ROW_TILE, n)
    g2 = g[None, :]
    return pl.pallas_call(
        _final_norm_kernel,
        grid=(n // tm,),
        in_specs=[pl.BlockSpec((tm, d), lambda i: (i, 0)), pl.BlockSpec(g2.shape, lambda i: (0, 0))],
        out_specs=pl.BlockSpec((tm, d), lambda i: (i, 0)),
        out_shape=jax.ShapeDtypeStruct((n, d), F32),
        compiler_params=_params(("arbitrary",)),
        name="final_norm",
    )(h, g2)
```

```python
import functools

import jax
import jax.numpy as jnp
from jax import lax
from jax.experimental import pallas as pl
from jax.experimental.pallas import tpu as pltpu

F32 = jnp.float32
BF16 = jnp.bfloat16
I32 = jnp.int32

N_HEADS = 8
QK_NOPE_DIM = 128
QK_ROPE_DIM = 64
QK_DIM = QK_NOPE_DIM + QK_ROPE_DIM
V_HEAD_DIM = 128
Q_LORA_RANK = 512
KV_LORA_RANK = 256
ROPE_THETA = 10000.0
POOL_WINDOWS = (2, 4, 8, 16)
POOL_HALO = 16
N_EXPERTS = 8
RMS_EPS = 1e-6
LOG2_E = 1.4426950408889634

LANES = 128
SUBLANES = 8
VMEM_LIMIT = 56 * 1024 * 1024

ROW_TILE = 512
ATTN_TILE = 512
ATTN_HEADS = 8
FF_CHUNK = 1024
WEIGHT_STAGE_ROWS = 128
ROUTER_TILE = 512
MOE_ROWS = 512


def _rms(x, g):
    ms = jnp.mean(x * x, axis=-1, keepdims=True)
    return x * lax.rsqrt(ms + RMS_EPS) * g


def _dot(a, b):
    return jnp.dot(a, b, preferred_element_type=F32)


def _dot_nt(a, b):
    return lax.dot_general(a, b, (((1,), (1,)), ((), ())), preferred_element_type=F32)


def _params(sem):
    return pltpu.CompilerParams(dimension_semantics=sem, vmem_limit_bytes=VMEM_LIMIT)


def _rows_to_tiles(ref, val):
    t = val.shape[0]
    for j in range(SUBLANES):
        ref[pl.ds(j, t, stride=SUBLANES), :] = val[:, j * LANES:(j + 1) * LANES]


def _tiles_to_rows(ref, t):
    return jnp.concatenate(
        [ref[pl.ds(j, t, stride=SUBLANES), :] for j in range(SUBLANES)], axis=-1)


def _rope_kernel(pos_ref, invf_ref, cos_ref, sin_ref):
    ang = pos_ref[...].astype(F32) * invf_ref[...]
    cos_ref[...] = jnp.cos(ang)
    sin_ref[...] = jnp.sin(ang)


def _rope_table(positions):
    n = positions.size
    half = QK_ROPE_DIM // 2
    per_row = LANES // half
    inv_freq = ROPE_THETA ** (-jnp.arange(0, QK_ROPE_DIM, 2, dtype=F32) / QK_ROPE_DIM)
    pos = jnp.repeat(positions.reshape(n // per_row, per_row), half, axis=1)
    invf = jnp.tile(inv_freq, per_row)[None, :]
    rows = n // per_row
    tr = min(rows, 2048)
    cos, sin = pl.pallas_call(
        _rope_kernel,
        grid=(rows // tr,),
        in_specs=[pl.BlockSpec((tr, LANES), lambda i: (i, 0)),
                  pl.BlockSpec((1, LANES), lambda i: (0, 0))],
        out_specs=[pl.BlockSpec((tr, LANES), lambda i: (i, 0)),
                   pl.BlockSpec((tr, LANES), lambda i: (i, 0))],
        out_shape=[jax.ShapeDtypeStruct((rows, LANES), F32)] * 2,
        compiler_params=_params(("arbitrary",)),
        name="rope_table",
    )(pos, invf)
    cos = cos.reshape(n, half)
    sin = sin.reshape(n, half)
    return jnp.concatenate([cos, cos, sin, sin], axis=-1)


def _mla_proj_kernel(h_ref, g_ref, wdn_ref, qn_ref, wq_ref, kvn_ref, wkv_ref, cs_ref,
                     q_ref, k_ref, v_ref, wd_ref, wuq_ref, wukv_ref):
    @pl.when(pl.program_id(0) == 0)
    def _():
        half = QK_ROPE_DIM // 2
        kr0 = Q_LORA_RANK + KV_LORA_RANK
        kr1 = kr0 + QK_ROPE_DIM
        wd_ref[:, 0:kr1] = wdn_ref[...].astype(BF16)
        wd_ref[:, kr1:kr1 + half] = (-wdn_ref[:, kr0 + half:kr1]).astype(BF16)
        wd_ref[:, kr1 + half:kr1 + 2 * half] = wdn_ref[:, kr0:kr0 + half].astype(BF16)
        rope0 = N_HEADS * QK_NOPE_DIM
        rot0 = rope0 + N_HEADS * QK_ROPE_DIM
        kvw = QK_NOPE_DIM + V_HEAD_DIM
        for hd in range(N_HEADS):
            b = hd * QK_DIM
            r = b + QK_NOPE_DIM
            wuq_ref[:, hd * QK_NOPE_DIM:(hd + 1) * QK_NOPE_DIM] = wq_ref[:, b:r].astype(BF16)
            wuq_ref[:, rope0 + hd * QK_ROPE_DIM:rope0 + (hd + 1) * QK_ROPE_DIM] = (
                wq_ref[:, r:r + QK_ROPE_DIM].astype(BF16))
            wuq_ref[:, rot0 + hd * QK_ROPE_DIM:rot0 + hd * QK_ROPE_DIM + half] = (
                -wq_ref[:, r + half:r + QK_ROPE_DIM]).astype(BF16)
            wuq_ref[:, rot0 + hd * QK_ROPE_DIM + half:rot0 + (hd + 1) * QK_ROPE_DIM] = (
                wq_ref[:, r:r + half].astype(BF16))
            wukv_ref[:, hd * QK_NOPE_DIM:(hd + 1) * QK_NOPE_DIM] = (
                wkv_ref[:, hd * kvw:hd * kvw + QK_NOPE_DIM].astype(BF16))
            wukv_ref[:, rope0 + hd * V_HEAD_DIM:rope0 + (hd + 1) * V_HEAD_DIM] = (
                wkv_ref[:, hd * kvw + QK_NOPE_DIM:(hd + 1) * kvw].astype(BF16))

    hn = _rms(h_ref[...], g_ref[...]).astype(BF16)
    down = _dot(hn, wd_ref[...])
    kv0 = Q_LORA_RANK
    kr0 = Q_LORA_RANK + KV_LORA_RANK
    cq = _rms(down[:, :kv0], qn_ref[...]).astype(BF16)
    ckv = _rms(down[:, kv0:kr0], kvn_ref[...]).astype(BF16)
    cs = cs_ref[...]
    cos2 = cs[:, :QK_ROPE_DIM]
    sin2 = cs[:, QK_ROPE_DIM:]
    k_rope = (down[:, kr0:kr0 + QK_ROPE_DIM] * cos2
              + down[:, kr0 + QK_ROPE_DIM:kr0 + 2 * QK_ROPE_DIM] * sin2).astype(BF16)
    q = _dot(cq, wuq_ref[...])
    kv = _dot(ckv, wukv_ref[...])
    scale = QK_DIM ** -0.5 * LOG2_E
    rope0 = N_HEADS * QK_NOPE_DIM
    rot0 = rope0 + N_HEADS * QK_ROPE_DIM
    v0 = N_HEADS * QK_NOPE_DIM
    for hd in range(N_HEADS):
        q_nope = q[:, hd * QK_NOPE_DIM:(hd + 1) * QK_NOPE_DIM] * scale
        q_rope = (q[:, rope0 + hd * QK_ROPE_DIM:rope0 + (hd + 1) * QK_ROPE_DIM] * cos2
                  + q[:, rot0 + hd * QK_ROPE_DIM:rot0 + (hd + 1) * QK_ROPE_DIM] * sin2) * scale
        q_ref[0, hd, :, 0:QK_NOPE_DIM] = q_nope.astype(BF16)
        q_ref[0, hd, :, QK_NOPE_DIM:QK_DIM] = q_rope.astype(BF16)
        k_ref[0, hd, :, 0:QK_NOPE_DIM] = kv[:, hd * QK_NOPE_DIM:(hd + 1) * QK_NOPE_DIM].astype(BF16)
        k_ref[0, hd, :, QK_NOPE_DIM:QK_DIM] = k_rope
        v_ref[0, hd] = kv[:, v0 + hd * V_HEAD_DIM:v0 + (hd + 1) * V_HEAD_DIM].astype(BF16)


def _mla_proj(h, g, w_down, q_norm, w_uq, kv_norm, w_ukv, a, cs, batch, seq):
    n, d = h.shape
    tm = min(ROW_TILE, seq)
    per_b = seq // tm
    full = lambda x: pl.BlockSpec(x.shape, lambda i: (0,) * x.ndim)
    layer = lambda x: pl.BlockSpec((None,) + x.shape[1:], lambda i: (a, 0, 0))
    head_spec = lambda w: pl.BlockSpec((1, N_HEADS, tm, w), lambda i: (i // per_b, 0, i % per_b, 0))
    g2, qn2, kvn2 = g[None, :], q_norm[None, :], kv_norm[None, :]
    kr1 = Q_LORA_RANK + KV_LORA_RANK + QK_ROPE_DIM
    return pl.pallas_call(
        _mla_proj_kernel,
        grid=(n // tm,),
        in_specs=[pl.BlockSpec((tm, d), lambda i: (i, 0)), full(g2), layer(w_down), full(qn2),
                  layer(w_uq), full(kvn2), layer(w_ukv),
                  pl.BlockSpec((tm, LANES), lambda i: (i, 0))],
        out_specs=[head_spec(QK_DIM), head_spec(QK_DIM), head_spec(V_HEAD_DIM)],
        out_shape=[jax.ShapeDtypeStruct((batch, N_HEADS, seq, QK_DIM), BF16),
                   jax.ShapeDtypeStruct((batch, N_HEADS, seq, QK_DIM), BF16),
                   jax.ShapeDtypeStruct((batch, N_HEADS, seq, V_HEAD_DIM), BF16)],
        scratch_shapes=[pltpu.VMEM((d, kr1 + QK_ROPE_DIM), BF16),
                        pltpu.VMEM((Q_LORA_RANK, N_HEADS * (QK_NOPE_DIM + 2 * QK_ROPE_DIM)), BF16),
                        pltpu.VMEM((KV_LORA_RANK, N_HEADS * (QK_NOPE_DIM + V_HEAD_DIM)), BF16)],
        compiler_params=_params(("arbitrary",)),
        name="mla_proj",
    )(h, g2, w_down, qn2, w_uq, kvn2, w_ukv, cs)


def _attn_kernel(q_ref, k_ref, v_ref, o_ref, m_ref, l_ref, acc_ref):
    hb, t = q_ref.shape[1], q_ref.shape[2]
    qi = pl.program_id(2)
    m_ref[...] = jnp.full(m_ref.shape, -jnp.inf, F32)
    l_ref[...] = jnp.zeros(l_ref.shape, F32)
    acc_ref[...] = jnp.zeros(acc_ref.shape, F32)

    def update(hd, r0, nr, k0, nk, causal):
        rows = slice(r0, r0 + nr)
        s = _dot_nt(q_ref[0, hd, rows, :], k_ref[0, hd, pl.ds(k0, nk), :])
        if causal:
            row = lax.broadcasted_iota(I32, s.shape, 0)
            col = lax.broadcasted_iota(I32, s.shape, 1)
            s = jnp.where(row >= col, s, -jnp.inf)
        m_prev = m_ref[hd, rows]
        m_new = jnp.maximum(m_prev, jnp.max(s, axis=1, keepdims=True))
        p = jnp.exp2(s - jnp.tile(m_new, (1, nk // LANES)))
        alpha = jnp.exp2(m_prev - m_new)
        l_ref[hd, rows] = alpha * l_ref[hd, rows] + jnp.sum(p, axis=1, keepdims=True)
        acc_ref[hd, rows] = alpha * acc_ref[hd, rows] + _dot(p.astype(BF16),
                                                             v_ref[0, hd, pl.ds(k0, nk), :])
        m_ref[hd, rows] = m_new

    def full_tile(ki, carry):
        k0 = pl.multiple_of(ki * t, t)
        for hd in range(hb):
            update(hd, 0, t, k0, t, False)
        return carry

    lax.fori_loop(0, qi, full_tile, 0)
    d0 = pl.multiple_of(qi * t, t)
    half = t // 2
    for hd in range(hb):
        update(hd, 0, t, d0, half, True)
        update(hd, half, half, pl.multiple_of(d0 + half, half), half, True)
    for hd in range(hb):
        o_ref[0, :, hd * V_HEAD_DIM:(hd + 1) * V_HEAD_DIM] = (
            acc_ref[hd] / l_ref[hd]).astype(o_ref.dtype)


def _attention(q, k, v):
    b, nh, s, _ = q.shape
    t = min(ATTN_TILE, s)
    hb = ATTN_HEADS
    kv_spec = lambda w: pl.BlockSpec((1, hb, s, w), lambda bi, gi, qi: (bi, gi, 0, 0))
    stat = pltpu.VMEM((hb, t, LANES), F32)
    return pl.pallas_call(
        _attn_kernel,
        grid=(b, nh // hb, s // t),
        in_specs=[pl.BlockSpec((1, hb, t, QK_DIM), lambda bi, gi, qi: (bi, gi, qi, 0)),
                  kv_spec(QK_DIM), kv_spec(V_HEAD_DIM)],
        out_specs=pl.BlockSpec((1, t, hb * V_HEAD_DIM), lambda bi, gi, qi: (bi, qi, gi)),
        out_shape=jax.ShapeDtypeStruct((b, s, nh * V_HEAD_DIM), BF16),
        scratch_shapes=[stat, stat, pltpu.VMEM((hb, t, V_HEAD_DIM), F32)],
        compiler_params=_params(("arbitrary",) * 3),
        name="mla_attention",
    )(q, k, v)


def _load_as_bf16(src_ref, dst_ref, stage_ref, sem):
    rc = stage_ref.shape[1]
    n_chunks = src_ref.shape[0] // rc

    def copy(k):
        return pltpu.make_async_copy(src_ref.at[pl.ds(k * rc, rc)], stage_ref.at[k % 2], sem.at[k % 2])

    copy(0).start()
    for k in range(n_chunks):
        if k + 1 < n_chunks:
            copy(k + 1).start()
        copy(k).wait()
        dst_ref[k * rc:(k + 1) * rc, :] = stage_ref[k % 2].astype(BF16)


def _oproj_ffn_kernel(h_ref, o_ref, g_ref, wo_hbm, wg_hbm, wu_hbm, wd_hbm, out_ref,
                      wo_ref, wg_ref, wu_ref, wd_ref, stage_d, stage_ff, sem, *, a, chunk):
    @pl.when(pl.program_id(0) == 0)
    def _():
        _load_as_bf16(wo_hbm.at[a], wo_ref, stage_d, sem)
        _load_as_bf16(wg_hbm.at[a], wg_ref, stage_ff, sem)
        _load_as_bf16(wu_hbm.at[a], wu_ref, stage_ff, sem)
        _load_as_bf16(wd_hbm.at[a], wd_ref, stage_d, sem)

    h1 = h_ref[...] + _dot(o_ref[...], wo_ref[...])
    hn = _rms(h1, g_ref[...]).astype(BF16)
    acc = h1
    ff = wg_ref.shape[1]
    for c0 in range(0, ff, chunk):
        c1 = min(c0 + chunk, ff)
        gate = _dot(hn, wg_ref[:, c0:c1])
        up = _dot(hn, wu_ref[:, c0:c1])
        act = (gate * jax.nn.sigmoid(gate) * up).astype(BF16)
        acc = acc + _dot(act, wd_ref[c0:c1, :])
    out_ref[...] = acc


def _oproj_ffn(h, o, g, wo, wg, wu, wd, a):
    n, d = h.shape
    dv, ff = wo.shape[1], wg.shape[2]
    tm = min(ROW_TILE, n)
    g2 = g[None, :]
    hbm = pl.BlockSpec(memory_space=pl.ANY)
    return pl.pallas_call(
        functools.partial(_oproj_ffn_kernel, a=a, chunk=FF_CHUNK),
        grid=(n // tm,),
        in_specs=[pl.BlockSpec((tm, d), lambda i: (i, 0)),
                  pl.BlockSpec((tm, dv), lambda i: (i, 0)),
                  pl.BlockSpec(g2.shape, lambda i: (0, 0)), hbm, hbm, hbm, hbm],
        out_specs=pl.BlockSpec((tm, d), lambda i: (i, 0)),
        out_shape=jax.ShapeDtypeStruct((n, d), F32),
        scratch_shapes=[pltpu.VMEM((dv, d), BF16), pltpu.VMEM((d, ff), BF16),
                        pltpu.VMEM((d, ff), BF16), pltpu.VMEM((ff, d), BF16),
                        pltpu.VMEM((2, WEIGHT_STAGE_ROWS, d), F32),
                        pltpu.VMEM((2, WEIGHT_STAGE_ROWS, ff), F32),
                        pltpu.SemaphoreType.DMA((2,))],
        compiler_params=_params(("arbitrary",)),
        name="oproj_ffn",
    )(h, o, g2, wo, wg, wu, wd)


def _pool_kernel(h_ref, g_ref, w_ref, sc_ref, out_ref, pad_ref):
    s, d = h_ref.shape[1], h_ref.shape[2]
    gd = d // len(POOL_WINDOWS)
    h = h_ref[0]
    hn = _rms(h, g_ref[...])
    t1 = lax.broadcasted_iota(I32, (s, 1), 0) + 1
    pad_ref[0:POOL_HALO, :] = jnp.zeros((POOL_HALO, gd), F32)
    for gi, w in enumerate(POOL_WINDOWS):
        x = hn[:, gi * gd:(gi + 1) * gd]
        acc = x
        k = 1
        while k < w:
            pad_ref[POOL_HALO:POOL_HALO + s, :] = acc
            acc = acc + pad_ref[POOL_HALO - k:POOL_HALO - k + s, :]
            k *= 2
        cnt = jnp.minimum(t1, w).astype(F32)
        z = (acc / cnt - x).astype(BF16)
        y = _dot(z, w_ref[gi]) * sc_ref[:, gi * gd:(gi + 1) * gd]
        out_ref[0, :, gi * gd:(gi + 1) * gd] = h[:, gi * gd:(gi + 1) * gd] + y


def _pool(h3, g, pool_w, pool_scale):
    b, s, d = h3.shape
    gd = d // len(POOL_WINDOWS)
    full = lambda a: pl.BlockSpec(a.shape, lambda i: (0,) * a.ndim)
    g2, sc2 = g[None, :], pool_scale[None, :]
    return pl.pallas_call(
        _pool_kernel,
        grid=(b,),
        in_specs=[pl.BlockSpec((1, s, d), lambda i: (i, 0, 0)), full(g2), full(pool_w), full(sc2)],
        out_specs=pl.BlockSpec((1, s, d), lambda i: (i, 0, 0)),
        out_shape=jax.ShapeDtypeStruct((b, s, d), F32),
        scratch_shapes=[pltpu.VMEM((s + POOL_HALO, gd), F32)],
        compiler_params=_params(("arbitrary",)),
        name="pool_mixer",
    )(h3, g2, pool_w, sc2)


def _row(ref, r):
    return ref.at[pl.ds(pl.multiple_of(r * SUBLANES, SUBLANES), SUBLANES)]


def _rows(ref, r, n_rows):
    return ref.at[pl.ds(pl.multiple_of(r * SUBLANES, SUBLANES), n_rows * SUBLANES)]


def _segment_copies(n_rows, start_copy, max_rows):
    for b in range(max_rows.bit_length()):
        size = 1 << b

        @pl.when(((n_rows >> b) & 1) == 1)
        def _():
            start_copy(n_rows & (size - 1), size)


SEG_STRIDE = 2 * N_EXPERTS


def _router_kernel(h_ref, g_ref, rt_ref, tri_ref, meta_ref, gate_ref, seg_ref, carry_ref):
    t = h_ref.shape[0]

    @pl.when(pl.program_id(0) == 0)
    def _():
        carry_ref[...] = jnp.zeros(carry_ref.shape, F32)

    hn = _rms(h_ref[...], g_ref[...])

    hi = hn.astype(BF16)
    lo = (hn - hi.astype(F32)).astype(BF16)
    rt = rt_ref[...]
    a = _dot_nt(rt, hi)
    logits = a[:N_EXPERTS] + a[N_EXPERTS:] + _dot_nt(rt[:N_EXPERTS], lo)

    eidx = lax.broadcasted_iota(I32, logits.shape, 0)
    m1 = jnp.max(logits, axis=0, keepdims=True)
    i1 = jnp.min(jnp.where(logits == m1, eidx, N_EXPERTS), axis=0, keepdims=True)
    oh1 = eidx == i1
    rest = jnp.where(oh1, -jnp.inf, logits)
    m2 = jnp.max(rest, axis=0, keepdims=True)
    i2 = jnp.min(jnp.where(rest == m2, eidx, N_EXPERTS), axis=0, keepdims=True)
    oh2 = eidx == i2
    e2 = jnp.exp(m2 - m1)
    den = 1.0 + e2
    g1 = 1.0 / den
    g2 = e2 / den

    sel = jnp.where(oh1 | oh2, 1.0, 0.0).astype(F32)
    pref = _dot(sel.astype(BF16), tri_ref[...])
    carry = carry_ref[...]
    cnt = jnp.broadcast_to(jnp.sum(sel, axis=1, keepdims=True), (N_EXPERTS, LANES))
    erow = lax.broadcasted_iota(I32, cnt.shape, 0)
    incl = cnt
    for sh in (1, 2, 4):
        incl = incl + jnp.where(erow >= sh, pltpu.roll(incl, sh, axis=0), 0.0)
    lrow = (incl - cnt)[:, 0:1] + pref
    ls1 = jnp.sum(jnp.where(oh1, lrow, 0.0), axis=0, keepdims=True).astype(I32)
    ls2 = jnp.sum(jnp.where(oh2, lrow, 0.0), axis=0, keepdims=True).astype(I32)
    carry_ref[...] = carry + cnt

    zi = jnp.zeros((SUBLANES - 4, t), I32)
    meta_ref[...] = jnp.concatenate([i1, i2, ls1, ls2, zi], axis=0)
    zf = jnp.zeros((SUBLANES - 2, t), F32)
    gate_ref[...] = jnp.concatenate([g1, g2, zf], axis=0)
    seg_ref[0] = jnp.concatenate([cnt, carry], axis=0).astype(I32)


def _router(h, g, router):
    n, d = h.shape
    t = min(ROUTER_TILE, n)
    r_t = router.T
    r_hi = r_t.astype(BF16)
    r_lo = (r_t - r_hi.astype(F32)).astype(BF16)
    rt = jnp.concatenate([r_hi, r_lo], axis=0)
    tri = jnp.triu(jnp.ones((t, t), BF16), k=1)
    full = lambda a: pl.BlockSpec(a.shape, lambda i: (0,) * a.ndim)
    g2 = g[None, :]
    return pl.pallas_call(
        _router_kernel,
        grid=(n // t,),
        in_specs=[pl.BlockSpec((t, d), lambda i: (i, 0)), full(g2), full(rt), full(tri)],
        out_specs=[pl.BlockSpec((SUBLANES, t), lambda i: (0, i)),
                   pl.BlockSpec((SUBLANES, t), lambda i: (0, i)),
                   pl.BlockSpec((1, SEG_STRIDE, LANES), lambda i: (i, 0, 0))],
        out_shape=[jax.ShapeDtypeStruct((SUBLANES, n), I32),
                   jax.ShapeDtypeStruct((SUBLANES, n), F32),
                   jax.ShapeDtypeStruct((n // t, SEG_STRIDE, LANES), I32)],
        scratch_shapes=[pltpu.VMEM((N_EXPERTS, LANES), F32)],
        compiler_params=_params(("arbitrary",)),
        name="moe_router",
    )(h, g2, rt, tri)


def _dispatch_kernel(seg_ref, zb_ref, h_ref, g_ref, mrow_ref, xb_ref, buf_ref, zero_ref, sem, zsem,
                     *, n_zero):
    t = h_ref.shape[0]
    i = pl.program_id(0)
    steps = pl.num_programs(0)
    slot = lax.rem(i, 2)

    def wait_rows(n_rows, s):
        span = _rows(xb_ref, 0, n_rows)
        pltpu.make_async_copy(span, span, s).wait()

    @pl.when(i == 0)
    def _():
        zero_ref[...] = jnp.zeros(zero_ref.shape, F32)
        for e in range(N_EXPERTS):
            def zstart(r, c):
                pltpu.make_async_copy(zero_ref, _row(xb_ref, r), zsem).start()
                return c
            lax.fori_loop(zb_ref[e], zb_ref[N_EXPERTS + e], zstart, 0)

    hi = _rms(h_ref[...], g_ref[...]).astype(BF16)
    srow = lax.broadcasted_iota(I32, (2 * t, t), 0)
    perm = jnp.where((srow == mrow_ref[2:3, :]) | (srow == mrow_ref[3:4, :]), 1.0, 0.0).astype(BF16)
    xs = _dot(perm, hi)

    @pl.when(i >= 2)
    def _():
        wait_rows(2 * t, sem.at[slot])
    buf = buf_ref.at[slot]
    _rows_to_tiles(buf, xs)

    l_e = 0
    for e in range(N_EXPERTS):
        c_e = seg_ref[i * SEG_STRIDE + e]

        def start_copy(off, size, src=l_e, dst=seg_ref[i * SEG_STRIDE + N_EXPERTS + e]):
            pltpu.make_async_copy(_rows(buf, src + off, size), _rows(xb_ref, dst + off, size),
                                  sem.at[slot]).start()
        _segment_copies(c_e, start_copy, t)
        l_e = l_e + c_e

    @pl.when(i == steps - 1)
    def _():
        wait_rows(2 * t, sem.at[slot])

        @pl.when(steps > 1)
        def _():
            wait_rows(2 * t, sem.at[1 - slot])
        wait_rows(n_zero, zsem)


def _dispatch(h, g, meta, seg, zero_bounds, n_rows):
    n, d = h.shape
    t = min(ROUTER_TILE, n)
    g2 = g[None, :]
    return pl.pallas_call(
        functools.partial(_dispatch_kernel, n_zero=n_rows - 2 * n),
        grid_spec=pltpu.PrefetchScalarGridSpec(
            num_scalar_prefetch=2,
            grid=(n // t,),
            in_specs=[pl.BlockSpec((t, d), lambda i, sg, zb: (i, 0)),
                      pl.BlockSpec(g2.shape, lambda i, sg, zb: (0, 0)),
                      pl.BlockSpec((SUBLANES, t), lambda i, sg, zb: (0, i))],
            out_specs=pl.BlockSpec(memory_space=pl.ANY),
            scratch_shapes=[pltpu.VMEM((2, 2 * t * SUBLANES, LANES), F32),
                            pltpu.VMEM((SUBLANES, LANES), F32),
                            pltpu.SemaphoreType.DMA((2,)),
                            pltpu.SemaphoreType.DMA],
        ),
        out_shape=jax.ShapeDtypeStruct((n_rows * SUBLANES, LANES), F32),
        compiler_params=_params(("arbitrary",)),
        name="moe_dispatch",
    )(seg, zero_bounds, h, g2, meta)


def _expert_kernel(be_ref, x_ref, wg_ref, wu_ref, wd_ref, y_ref, wgu_s, wd_s):
    i = pl.program_id(0)
    ff = wg_ref.shape[2]

    @pl.when((i == 0) | (be_ref[i] != be_ref[jnp.maximum(i - 1, 0)]))
    def _():
        for j in range(wgu_s.shape[1] // (2 * LANES)):
            c0 = j * LANES
            w = min(LANES, ff - c0)
            for half, ref in ((0, wg_ref), (1, wu_ref)):
                s0 = (2 * j + half) * LANES
                wgu_s[:, s0:s0 + w] = ref[0, :, c0:c0 + w].astype(BF16)
                if w < LANES:
                    wgu_s[:, s0 + w:s0 + LANES] = jnp.zeros((wgu_s.shape[0], LANES - w), BF16)
        wd_s[0:ff, :] = wd_ref[0].astype(BF16)
        if wd_s.shape[0] > ff:
            wd_s[ff:, :] = jnp.zeros((wd_s.shape[0] - ff, wd_s.shape[1]), BF16)

    bm = x_ref.shape[0] // SUBLANES
    x = _tiles_to_rows(x_ref, bm).astype(BF16)
    gu = _dot(x, wgu_s[...])
    acts = []
    for c0 in range(0, gu.shape[1], 2 * LANES):
        gate = gu[:, c0:c0 + LANES]
        up = gu[:, c0 + LANES:c0 + 2 * LANES]
        acts.append((gate * jax.nn.sigmoid(gate) * up).astype(BF16))
    act = jnp.concatenate(acts, axis=-1)
    _rows_to_tiles(y_ref, _dot(act, wd_s[...]))


def _experts(xb, block_e, wg, wu, wd, a, bm):
    d, ff = wg.shape[2], wg.shape[3]
    ffp = -(-ff // LANES) * LANES
    once = pl.Buffered(1)
    w_spec = lambda r, c: pl.BlockSpec((None, 1, r, c), lambda i, be: (a, be[i], 0, 0),
                                       pipeline_mode=once)
    return pl.pallas_call(
        _expert_kernel,
        grid_spec=pltpu.PrefetchScalarGridSpec(
            num_scalar_prefetch=1,
            grid=(block_e.shape[0],),
            in_specs=[pl.BlockSpec((bm * SUBLANES, LANES), lambda i, be: (i, 0)),
                      w_spec(d, ff), w_spec(d, ff), w_spec(ff, d)],
            out_specs=pl.BlockSpec((bm * SUBLANES, LANES), lambda i, be: (i, 0)),
            scratch_shapes=[pltpu.VMEM((d, 2 * ffp), BF16), pltpu.VMEM((ffp, d), BF16)],
        ),
        out_shape=jax.ShapeDtypeStruct(xb.shape, F32),
        compiler_params=_params(("arbitrary",)),
        name="moe_experts",
    )(block_e, xb, wg, wu, wd)


def _combine_kernel(seg_ref, h_ref, mrow_ref, grow_ref, mcol_ref, fn_ref, yb_ref, out_ref,
                    buf_ref, sem, *, final_norm):
    t = h_ref.shape[0]
    i = pl.program_id(0)
    steps = pl.num_programs(0)
    slot = lax.rem(i, 2)

    def fetch(tile, sl):
        buf = buf_ref.at[sl]
        l_e = 0
        for e in range(N_EXPERTS):
            c_e = seg_ref[tile * SEG_STRIDE + e]

            def start_copy(off, size, src=seg_ref[tile * SEG_STRIDE + N_EXPERTS + e], dst=l_e):
                pltpu.make_async_copy(_rows(yb_ref, src + off, size), _rows(buf, dst + off, size),
                                      sem.at[sl]).start()
            _segment_copies(c_e, start_copy, t)
            l_e = l_e + c_e

    @pl.when(i == 0)
    def _():
        fetch(0, 0)

    @pl.when(i + 1 < steps)
    def _():
        fetch(i + 1, 1 - slot)

    buf = buf_ref.at[slot]
    pltpu.make_async_copy(_rows(yb_ref, 0, 2 * t), buf, sem.at[slot]).wait()
    ys = _tiles_to_rows(buf, 2 * t)

    ls1 = mrow_ref[2:3, :]
    ls2 = mrow_ref[3:4, :]
    srow = lax.broadcasted_iota(I32, (2 * t, t), 0)
    w = (jnp.where(srow == ls1, grow_ref[0:1, :], 0.0)
         + jnp.where(srow == ls2, grow_ref[1:2, :], 0.0))
    ysg = ys * jnp.sum(w, axis=1, keepdims=True)
    hi = ysg.astype(BF16)
    lo = (ysg - hi.astype(F32)).astype(BF16)
    scol = lax.broadcasted_iota(I32, (t, 2 * t), 1)
    unsort = jnp.where((scol == mcol_ref[:, 0:1]) | (scol == mcol_ref[:, 1:2]), 1.0, 0.0).astype(BF16)
    out = h_ref[...] + _dot(unsort, hi) + _dot(unsort, lo)
    if final_norm:
        out = _rms(out, fn_ref[...])
    out_ref[...] = out


def _combine(h, yb, seg, meta, gates, fnorm, final_norm):
    n, d = h.shape
    t = min(ROUTER_TILE, n)
    mcol = meta[2:4].T
    fn2 = fnorm[None, :]
    return pl.pallas_call(
        functools.partial(_combine_kernel, final_norm=final_norm),
        grid_spec=pltpu.PrefetchScalarGridSpec(
            num_scalar_prefetch=1,
            grid=(n // t,),
            in_specs=[pl.BlockSpec((t, d), lambda i, sg: (i, 0)),
                      pl.BlockSpec((SUBLANES, t), lambda i, sg: (0, i)),
                      pl.BlockSpec((SUBLANES, t), lambda i, sg: (0, i)),
                      pl.BlockSpec((t, 2), lambda i, sg: (i, 0)),
                      pl.BlockSpec(fn2.shape, lambda i, sg: (0, 0)),
                      pl.BlockSpec(memory_space=pl.ANY)],
            out_specs=pl.BlockSpec((t, d), lambda i, sg: (i, 0)),
            scratch_shapes=[pltpu.VMEM((2, 2 * t * SUBLANES, LANES), F32),
                            pltpu.SemaphoreType.DMA((2,))],
        ),
        out_shape=jax.ShapeDtypeStruct((n, d), F32),
        compiler_params=_params(("arbitrary",)),
        name="moe_combine",
    )(seg, h, meta, gates, mcol, fn2, yb)


def _moe(h, g, router, wg, wu, wd, a, fnorm, final_norm):
    n = h.shape[0]
    bm = min(MOE_ROWS, n)
    meta, gates, segv = _router(h, g, router)
    tile_cnt = segv[:, :N_EXPERTS, 0]
    tile_base = segv[:, N_EXPERTS:, 0]
    counts = tile_cnt[-1] + tile_base[-1]
    padded = (counts + bm - 1) // bm * bm
    pad_end = jnp.cumsum(padded)
    pad_start = pad_end - padded
    n_blocks = (n * 2) // bm + N_EXPERTS
    n_rows = n_blocks * bm
    seg = jnp.concatenate([tile_cnt, tile_base + pad_start[None, :]], axis=1).reshape(-1).astype(I32)
    zero_bounds = jnp.concatenate([pad_start + counts, pad_start[1:],
                                   jnp.array([n_rows], I32)]).astype(I32)
    first_row = jnp.arange(n_blocks, dtype=I32)[:, None] * bm
    block_e = jnp.minimum(jnp.sum((pad_end[None, :] <= first_row).astype(I32), axis=1),
                          N_EXPERTS - 1).astype(I32)
    xb = _dispatch(h, g, meta, seg, zero_bounds, n_rows)
    yb = _experts(xb, block_e, wg, wu, wd, a, bm)
    return _combine(h, yb, seg, meta, gates, fnorm, final_norm)


def kernel(x, positions, attn_norm, ffn_norm, mla_w_down, mla_q_norm, mla_w_uq, mla_kv_norm,
           mla_w_ukv, mla_w_o, pool_w, pool_scale, ffn_w_gate, ffn_w_up, ffn_w_down,
           moe_router, moe_w_gate, moe_w_up, moe_w_down, final_norm):
    b, s, d = x.shape
    n = b * s
    depth = attn_norm.shape[0]
    h = x.reshape(n, d)
    cs = _rope_table(positions)
    for layer in range(depth):
        a = layer // 2
        last = layer == depth - 1
        if layer % 2 == 0:
            q, k, v = _mla_proj(h, attn_norm[layer], mla_w_down, mla_q_norm[a], mla_w_uq,
                                mla_kv_norm[a], mla_w_ukv, a, cs, b, s)
            o = _attention(q, k, v).reshape(n, N_HEADS * V_HEAD_DIM)
            h = _oproj_ffn(h, o, ffn_norm[layer], mla_w_o, ffn_w_gate, ffn_w_up, ffn_w_down, a)
            if last:
                h = _final_norm(h, final_norm)
        else:
            h = _pool(h.reshape(b, s, d), attn_norm[layer], pool_w[a].astype(BF16),
                      pool_scale[a]).reshape(n, d)
            h = _moe(h, ffn_norm[layer], moe_router[a], moe_w_gate, moe_w_up, moe_w_down, a,
                     final_norm, last)
    return h.reshape(b, s, d)


def _final_norm_kernel(h_ref, g_ref, o_ref):
    o_ref[...] = _rms(h_ref[...], g_ref[...])


def _final_norm(h, g):
    n, d = h.shape
    tm = min(ROW_TILE, n)
    g2 = g[None, :]
    return pl.pallas_call(
        _final_norm_kernel,
        grid=(n // tm,),
        in_specs=[pl.BlockSpec((tm, d), lambda i: (i, 0)), pl.BlockSpec(g2.shape, lambda i: (0, 0))],
        out_specs=pl.BlockSpec((tm, d), lambda i: (i, 0)),
        out_shape=jax.ShapeDtypeStruct((n, d), F32),
        compiler_params=_params(("arbitrary",)),
        name="final_norm",
    )(h, g2)
```

```python
import functools

import jax
import jax.numpy as jnp
from jax import lax
from jax.experimental import pallas as pl
from jax.experimental.pallas import tpu as pltpu

F32 = jnp.float32
BF16 = jnp.bfloat16
I32 = jnp.int32

N_HEADS = 8
QK_NOPE_DIM = 128
QK_ROPE_DIM = 64
QK_DIM = QK_NOPE_DIM + QK_ROPE_DIM
V_HEAD_DIM = 128
V_ONES = 128
Q_LORA_RANK = 512
KV_LORA_RANK = 256
ROPE_THETA = 10000.0
POOL_WINDOWS = (2, 4, 8, 16)
POOL_HALO = 16
N_EXPERTS = 8
RMS_EPS = 1e-6
LOG2_E = 1.4426950408889634

LANES = 128
SUBLANES = 8
VMEM_LIMIT = 56 * 1024 * 1024

ROW_TILE = 512
ATTN_TILE = 512
ATTN_HEADS = 8
FF_CHUNK = 1024
WEIGHT_STAGE_ROWS = 128
ROUTER_TILE = 512
MOE_ROWS = 512


def _rms(x, g):
    ms = jnp.mean(x * x, axis=-1, keepdims=True)
    return x * lax.rsqrt(ms + RMS_EPS) * g


def _dot(a, b):
    return jnp.dot(a, b, preferred_element_type=F32)


def _dot_nt(a, b):
    return lax.dot_general(a, b, (((1,), (1,)), ((), ())), preferred_element_type=F32)


def _params(sem):
    return pltpu.CompilerParams(dimension_semantics=sem, vmem_limit_bytes=VMEM_LIMIT)


def _rows_to_tiles(ref, val):
    t = val.shape[0]
    for j in range(SUBLANES):
        ref[pl.ds(j, t, stride=SUBLANES), :] = val[:, j * LANES:(j + 1) * LANES]


def _tiles_to_rows(ref, t):
    return jnp.concatenate(
        [ref[pl.ds(j, t, stride=SUBLANES), :] for j in range(SUBLANES)], axis=-1)


def _rope_kernel(pos_ref, invf_ref, cos_ref, sin_ref):
    ang = pos_ref[...].astype(F32) * invf_ref[...]
    cos_ref[...] = jnp.cos(ang)
    sin_ref[...] = jnp.sin(ang)


def _rope_table(positions):
    n = positions.size
    half = QK_ROPE_DIM // 2
    per_row = LANES // half
    inv_freq = ROPE_THETA ** (-jnp.arange(0, QK_ROPE_DIM, 2, dtype=F32) / QK_ROPE_DIM)
    pos = jnp.repeat(positions.reshape(n // per_row, per_row), half, axis=1)
    invf = jnp.tile(inv_freq, per_row)[None, :]
    rows = n // per_row
    tr = min(rows, 2048)
    cos, sin = pl.pallas_call(
        _rope_kernel,
        grid=(rows // tr,),
        in_specs=[pl.BlockSpec((tr, LANES), lambda i: (i, 0)),
                  pl.BlockSpec((1, LANES), lambda i: (0, 0))],
        out_specs=[pl.BlockSpec((tr, LANES), lambda i: (i, 0)),
                   pl.BlockSpec((tr, LANES), lambda i: (i, 0))],
        out_shape=[jax.ShapeDtypeStruct((rows, LANES), F32)] * 2,
        compiler_params=_params(("arbitrary",)),
        name="rope_table",
    )(pos, invf)
    cos = cos.reshape(n, half)
    sin = sin.reshape(n, half)
    return jnp.concatenate([cos, cos, sin, sin], axis=-1)


def _mla_proj_kernel(h_ref, g_ref, wdn_ref, qn_ref, wq_ref, kvn_ref, wkv_ref, cs_ref,
                     q_ref, k_ref, v_ref, wd_ref, wuq_ref, wukv_ref):
    @pl.when(pl.program_id(0) == 0)
    def _():
        half = QK_ROPE_DIM // 2
        kr0 = Q_LORA_RANK + KV_LORA_RANK
        kr1 = kr0 + QK_ROPE_DIM
        wd_ref[:, 0:kr1] = wdn_ref[...].astype(BF16)
        wd_ref[:, kr1:kr1 + half] = (-wdn_ref[:, kr0 + half:kr1]).astype(BF16)
        wd_ref[:, kr1 + half:kr1 + 2 * half] = wdn_ref[:, kr0:kr0 + half].astype(BF16)
        rope0 = N_HEADS * QK_NOPE_DIM
        rot0 = rope0 + N_HEADS * QK_ROPE_DIM
        kvw = QK_NOPE_DIM + V_HEAD_DIM
        for hd in range(N_HEADS):
            b = hd * QK_DIM
            r = b + QK_NOPE_DIM
            wuq_ref[:, hd * QK_NOPE_DIM:(hd + 1) * QK_NOPE_DIM] = wq_ref[:, b:r].astype(BF16)
            wuq_ref[:, rope0 + hd * QK_ROPE_DIM:rope0 + (hd + 1) * QK_ROPE_DIM] = (
                wq_ref[:, r:r + QK_ROPE_DIM].astype(BF16))
            wuq_ref[:, rot0 + hd * QK_ROPE_DIM:rot0 + hd * QK_ROPE_DIM + half] = (
                -wq_ref[:, r + half:r + QK_ROPE_DIM]).astype(BF16)
            wuq_ref[:, rot0 + hd * QK_ROPE_DIM + half:rot0 + (hd + 1) * QK_ROPE_DIM] = (
                wq_ref[:, r:r + half].astype(BF16))
            wukv_ref[:, hd * QK_NOPE_DIM:(hd + 1) * QK_NOPE_DIM] = (
                wkv_ref[:, hd * kvw:hd * kvw + QK_NOPE_DIM].astype(BF16))
            wukv_ref[:, rope0 + hd * V_HEAD_DIM:rope0 + (hd + 1) * V_HEAD_DIM] = (
                wkv_ref[:, hd * kvw + QK_NOPE_DIM:(hd + 1) * kvw].astype(BF16))

    hn = _rms(h_ref[...], g_ref[...]).astype(BF16)
    down = _dot(hn, wd_ref[...])
    kv0 = Q_LORA_RANK
    kr0 = Q_LORA_RANK + KV_LORA_RANK
    cq = _rms(down[:, :kv0], qn_ref[...]).astype(BF16)
    ckv = _rms(down[:, kv0:kr0], kvn_ref[...]).astype(BF16)
    cs = cs_ref[...]
    cos2 = cs[:, :QK_ROPE_DIM]
    sin2 = cs[:, QK_ROPE_DIM:]
    k_rope = (down[:, kr0:kr0 + QK_ROPE_DIM] * cos2
              + down[:, kr0 + QK_ROPE_DIM:kr0 + 2 * QK_ROPE_DIM] * sin2).astype(BF16)
    q = _dot(cq, wuq_ref[...])
    kv = _dot(ckv, wukv_ref[...])
    scale = QK_DIM ** -0.5 * LOG2_E
    rope0 = N_HEADS * QK_NOPE_DIM
    rot0 = rope0 + N_HEADS * QK_ROPE_DIM
    v0 = N_HEADS * QK_NOPE_DIM
    for hd in range(N_HEADS):
        q_nope = q[:, hd * QK_NOPE_DIM:(hd + 1) * QK_NOPE_DIM] * scale
        q_rope = (q[:, rope0 + hd * QK_ROPE_DIM:rope0 + (hd + 1) * QK_ROPE_DIM] * cos2
                  + q[:, rot0 + hd * QK_ROPE_DIM:rot0 + (hd + 1) * QK_ROPE_DIM] * sin2) * scale
        q_ref[0, hd, :, 0:QK_NOPE_DIM] = q_nope.astype(BF16)
        q_ref[0, hd, :, QK_NOPE_DIM:QK_DIM] = q_rope.astype(BF16)
        k_ref[0, hd, :, 0:QK_NOPE_DIM] = kv[:, hd * QK_NOPE_DIM:(hd + 1) * QK_NOPE_DIM].astype(BF16)
        k_ref[0, hd, :, QK_NOPE_DIM:QK_DIM] = k_rope
        v_ref[0, hd, :, 0:V_HEAD_DIM] = kv[:, v0 + hd * V_HEAD_DIM:v0 + (hd + 1) * V_HEAD_DIM].astype(BF16)
        v_ref[0, hd, :, V_HEAD_DIM:] = jnp.ones((kv.shape[0], V_ONES), BF16)


def _mla_proj(h, g, w_down, q_norm, w_uq, kv_norm, w_ukv, a, cs, batch, seq):
    n, d = h.shape
    tm = min(ROW_TILE, seq)
    per_b = seq // tm
    full = lambda x: pl.BlockSpec(x.shape, lambda i: (0,) * x.ndim)
    layer = lambda x: pl.BlockSpec((None,) + x.shape[1:], lambda i: (a, 0, 0))
    head_spec = lambda w: pl.BlockSpec((1, N_HEADS, tm, w), lambda i: (i // per_b, 0, i % per_b, 0))
    g2, qn2, kvn2 = g[None, :], q_norm[None, :], kv_norm[None, :]
    kr1 = Q_LORA_RANK + KV_LORA_RANK + QK_ROPE_DIM
    return pl.pallas_call(
        _mla_proj_kernel,
        grid=(n // tm,),
        in_specs=[pl.BlockSpec((tm, d), lambda i: (i, 0)), full(g2), layer(w_down), full(qn2),
                  layer(w_uq), full(kvn2), layer(w_ukv),
                  pl.BlockSpec((tm, LANES), lambda i: (i, 0))],
        out_specs=[head_spec(QK_DIM), head_spec(QK_DIM), head_spec(V_HEAD_DIM + V_ONES)],
        out_shape=[jax.ShapeDtypeStruct((batch, N_HEADS, seq, QK_DIM), BF16),
                   jax.ShapeDtypeStruct((batch, N_HEADS, seq, QK_DIM), BF16),
                   jax.ShapeDtypeStruct((batch, N_HEADS, seq, V_HEAD_DIM + V_ONES), BF16)],
        scratch_shapes=[pltpu.VMEM((d, kr1 + QK_ROPE_DIM), BF16),
                        pltpu.VMEM((Q_LORA_RANK, N_HEADS * (QK_NOPE_DIM + 2 * QK_ROPE_DIM)), BF16),
                        pltpu.VMEM((KV_LORA_RANK, N_HEADS * (QK_NOPE_DIM + V_HEAD_DIM)), BF16)],
        compiler_params=_params(("arbitrary",)),
        name="mla_proj",
    )(h, g2, w_down, qn2, w_uq, kvn2, w_ukv, cs)


def _attn_kernel(q_ref, k_ref, v_ref, o_ref, m_ref, acc_ref):
    hb, t = q_ref.shape[1], q_ref.shape[2]
    qi = pl.program_id(2)
    m_ref[...] = jnp.full(m_ref.shape, -jnp.inf, F32)
    acc_ref[...] = jnp.zeros(acc_ref.shape, F32)

    def update(hd, r0, nr, k0, nk, causal):
        rows = slice(r0, r0 + nr)
        s = _dot_nt(q_ref[0, hd, rows, :], k_ref[0, hd, pl.ds(k0, nk), :])
        if causal:
            row = lax.broadcasted_iota(I32, s.shape, 0)
            col = lax.broadcasted_iota(I32, s.shape, 1)
            s = jnp.where(row >= col, s, -jnp.inf)
        m_prev = m_ref[hd, rows]
        m_new = jnp.maximum(m_prev, jnp.max(s, axis=1, keepdims=True))
        p = jnp.exp2(s - jnp.tile(m_new, (1, nk // LANES)))
        alpha = jnp.exp2(m_prev - m_new)
        acc_ref[hd, rows] = (jnp.tile(alpha, (1, acc_ref.shape[2] // LANES)) * acc_ref[hd, rows]
                             + _dot(p.astype(BF16), v_ref[0, hd, pl.ds(k0, nk), :]))
        m_ref[hd, rows] = m_new

    def full_tile(ki, carry):
        k0 = pl.multiple_of(ki * t, t)
        for hd in range(hb):
            update(hd, 0, t, k0, t, False)
        return carry

    lax.fori_loop(0, qi, full_tile, 0)
    d0 = pl.multiple_of(qi * t, t)
    half = t // 2
    for hd in range(hb):
        update(hd, 0, t, d0, half, True)
        update(hd, half, half, pl.multiple_of(d0 + half, half), half, True)
    for hd in range(hb):
        o_ref[0, :, hd * V_HEAD_DIM:(hd + 1) * V_HEAD_DIM] = (
            acc_ref[hd, :, 0:V_HEAD_DIM] / acc_ref[hd, :, V_HEAD_DIM:V_HEAD_DIM + LANES]
        ).astype(o_ref.dtype)


def _attention(q, k, v):
    b, nh, s, _ = q.shape
    t = min(ATTN_TILE, s)
    hb = ATTN_HEADS
    kv_spec = lambda w: pl.BlockSpec((1, hb, s, w), lambda bi, gi, qi: (bi, gi, 0, 0))
    stat = pltpu.VMEM((hb, t, LANES), F32)
    return pl.pallas_call(
        _attn_kernel,
        grid=(b, nh // hb, s // t),
        in_specs=[pl.BlockSpec((1, hb, t, QK_DIM), lambda bi, gi, qi: (bi, gi, qi, 0)),
                  kv_spec(QK_DIM), kv_spec(v.shape[3])],
        out_specs=pl.BlockSpec((1, t, hb * V_HEAD_DIM), lambda bi, gi, qi: (bi, qi, gi)),
        out_shape=jax.ShapeDtypeStruct((b, s, nh * V_HEAD_DIM), BF16),
        scratch_shapes=[stat, pltpu.VMEM((hb, t, v.shape[3]), F32)],
        compiler_params=_params(("arbitrary",) * 3),
        name="mla_attention",
    )(q, k, v)


def _load_as_bf16(src_ref, dst_ref, stage_ref, sem):
    rc = stage_ref.shape[1]
    n_chunks = src_ref.shape[0] // rc

    def copy(k):
        return pltpu.make_async_copy(src_ref.at[pl.ds(k * rc, rc)], stage_ref.at[k % 2], sem.at[k % 2])

    copy(0).start()
    for k in range(n_chunks):
        if k + 1 < n_chunks:
            copy(k + 1).start()
        copy(k).wait()
        dst_ref[k * rc:(k + 1) * rc, :] = stage_ref[k % 2].astype(BF16)


def _oproj_ffn_kernel(h_ref, o_ref, g_ref, wo_hbm, wg_hbm, wu_hbm, wd_hbm, out_ref,
                      wo_ref, wg_ref, wu_ref, wd_ref, stage_d, stage_ff, sem, *, a, chunk):
    @pl.when(pl.program_id(0) == 0)
    def _():
        _load_as_bf16(wo_hbm.at[a], wo_ref, stage_d, sem)
        _load_as_bf16(wg_hbm.at[a], wg_ref, stage_ff, sem)
        _load_as_bf16(wu_hbm.at[a], wu_ref, stage_ff, sem)
        _load_as_bf16(wd_hbm.at[a], wd_ref, stage_d, sem)

    h1 = h_ref[...] + _dot(o_ref[...], wo_ref[...])
    hn = _rms(h1, g_ref[...]).astype(BF16)
    acc = h1
    ff = wg_ref.shape[1]
    for c0 in range(0, ff, chunk):
        c1 = min(c0 + chunk, ff)
        gate = _dot(hn, wg_ref[:, c0:c1])
        up = _dot(hn, wu_ref[:, c0:c1])
        act = (gate * jax.nn.sigmoid(gate) * up).astype(BF16)
        acc = acc + _dot(act, wd_ref[c0:c1, :])
    out_ref[...] = acc


def _oproj_ffn(h, o, g, wo, wg, wu, wd, a):
    n, d = h.shape
    dv, ff = wo.shape[1], wg.shape[2]
    tm = min(ROW_TILE, n)
    g2 = g[None, :]
    hbm = pl.BlockSpec(memory_space=pl.ANY)
    return pl.pallas_call(
        functools.partial(_oproj_ffn_kernel, a=a, chunk=FF_CHUNK),
        grid=(n // tm,),
        in_specs=[pl.BlockSpec((tm, d), lambda i: (i, 0)),
                  pl.BlockSpec((tm, dv), lambda i: (i, 0)),
                  pl.BlockSpec(g2.shape, lambda i: (0, 0)), hbm, hbm, hbm, hbm],
        out_specs=pl.BlockSpec((tm, d), lambda i: (i, 0)),
        out_shape=jax.ShapeDtypeStruct((n, d), F32),
        scratch_shapes=[pltpu.VMEM((dv, d), BF16), pltpu.VMEM((d, ff), BF16),
                        pltpu.VMEM((d, ff), BF16), pltpu.VMEM((ff, d), BF16),
                        pltpu.VMEM((2, WEIGHT_STAGE_ROWS, d), F32),
                        pltpu.VMEM((2, WEIGHT_STAGE_ROWS, ff), F32),
                        pltpu.SemaphoreType.DMA((2,))],
        compiler_params=_params(("arbitrary",)),
        name="oproj_ffn",
    )(h, o, g2, wo, wg, wu, wd)


def _row(ref, r):
    return ref.at[pl.ds(pl.multiple_of(r * SUBLANES, SUBLANES), SUBLANES)]


def _rows(ref, r, n_rows):
    return ref.at[pl.ds(pl.multiple_of(r * SUBLANES, SUBLANES), n_rows * SUBLANES)]


def _segment_copies(n_rows, start_copy, max_rows):
    for b in range(max_rows.bit_length()):
        size = 1 << b

        @pl.when(((n_rows >> b) & 1) == 1)
        def _():
            start_copy(n_rows & (size - 1), size)


SEG_STRIDE = 2 * N_EXPERTS


def _pool_router_kernel(h_ref, gp_ref, pw_ref, ps_ref, gf_ref, rt_ref, tri_ref,
                        out_ref, meta_ref, gate_ref, seg_ref, pad_ref, carry_ref):
    s, d = h_ref.shape[1], h_ref.shape[2]
    gd = d // len(POOL_WINDOWS)
    t = tri_ref.shape[0]

    @pl.when(pl.program_id(0) == 0)
    def _():
        carry_ref[...] = jnp.zeros(carry_ref.shape, F32)

    h = h_ref[0]
    hp = _rms(h, gp_ref[...])
    t1 = lax.broadcasted_iota(I32, (s, 1), 0) + 1
    pad_ref[0:POOL_HALO, :] = jnp.zeros((POOL_HALO, gd), F32)
    for gi, w in enumerate(POOL_WINDOWS):
        cols = slice(gi * gd, (gi + 1) * gd)
        x = hp[:, cols]
        run = x
        k = 1
        while k < w:
            pad_ref[POOL_HALO:POOL_HALO + s, :] = run
            run = run + pad_ref[POOL_HALO - k:POOL_HALO - k + s, :]
            k *= 2
        cnt = jnp.minimum(t1, w).astype(F32)
        z = (run / cnt - x).astype(BF16)
        y = _dot(z, pw_ref[gi].astype(BF16)) * ps_ref[:, cols]
        out_ref[0, :, cols] = h[:, cols] + y

    for j in range(s // t):
        tok = slice(j * t, (j + 1) * t)
        meta, gates, seg = _route_tile(out_ref[0, tok, :], gf_ref[...], rt_ref[...], tri_ref[...],
                                       carry_ref)
        meta_ref[:, tok] = meta
        gate_ref[:, tok] = gates
        seg_ref[j] = seg


def _route_tile(h, g, rt, tri, carry_ref):
    t = h.shape[0]
    hn = _rms(h, g)

    hi = hn.astype(BF16)
    lo = (hn - hi.astype(F32)).astype(BF16)
    a = _dot_nt(rt, hi)
    logits = a[:N_EXPERTS] + a[N_EXPERTS:] + _dot_nt(rt[:N_EXPERTS], lo)

    eidx = lax.broadcasted_iota(I32, logits.shape, 0)
    m1 = jnp.max(logits, axis=0, keepdims=True)
    i1 = jnp.min(jnp.where(logits == m1, eidx, N_EXPERTS), axis=0, keepdims=True)
    oh1 = eidx == i1
    rest = jnp.where(oh1, -jnp.inf, logits)
    m2 = jnp.max(rest, axis=0, keepdims=True)
    i2 = jnp.min(jnp.where(rest == m2, eidx, N_EXPERTS), axis=0, keepdims=True)
    oh2 = eidx == i2
    e2 = jnp.exp(m2 - m1)
    den = 1.0 + e2
    g1 = 1.0 / den
    g2 = e2 / den

    sel = jnp.where(oh1 | oh2, 1.0, 0.0).astype(F32)
    pref = _dot(sel.astype(BF16), tri)
    carry = carry_ref[...]
    cnt = jnp.broadcast_to(jnp.sum(sel, axis=1, keepdims=True), (N_EXPERTS, LANES))
    erow = lax.broadcasted_iota(I32, cnt.shape, 0)
    incl = cnt
    for sh in (1, 2, 4):
        incl = incl + jnp.where(erow >= sh, pltpu.roll(incl, sh, axis=0), 0.0)
    lrow = (incl - cnt)[:, 0:1] + pref
    ls1 = jnp.sum(jnp.where(oh1, lrow, 0.0), axis=0, keepdims=True).astype(I32)
    ls2 = jnp.sum(jnp.where(oh2, lrow, 0.0), axis=0, keepdims=True).astype(I32)
    carry_ref[...] = carry + cnt

    zi = jnp.zeros((SUBLANES - 4, t), I32)
    meta = jnp.concatenate([i1, i2, ls1, ls2, zi], axis=0)
    zf = jnp.zeros((SUBLANES - 2, t), F32)
    gates = jnp.concatenate([g1, g2, zf], axis=0)
    return meta, gates, jnp.concatenate([cnt, carry], axis=0).astype(I32)


def _pool_router(h3, g_pool, pool_w, pool_scale, g_ffn, router):
    b, s, d = h3.shape
    n = b * s
    gd = d // len(POOL_WINDOWS)
    t = min(ROUTER_TILE, s)
    r_t = router.T
    r_hi = r_t.astype(BF16)
    r_lo = (r_t - r_hi.astype(F32)).astype(BF16)
    rt = jnp.concatenate([r_hi, r_lo], axis=0)
    tri = jnp.triu(jnp.ones((t, t), BF16), k=1)
    full = lambda a: pl.BlockSpec(a.shape, lambda i: (0,) * a.ndim)
    gp2, ps2, gf2 = g_pool[None, :], pool_scale[None, :], g_ffn[None, :]
    return pl.pallas_call(
        _pool_router_kernel,
        grid=(b,),
        in_specs=[pl.BlockSpec((1, s, d), lambda i: (i, 0, 0)), full(gp2), full(pool_w), full(ps2),
                  full(gf2), full(rt), full(tri)],
        out_specs=[pl.BlockSpec((1, s, d), lambda i: (i, 0, 0)),
                   pl.BlockSpec((SUBLANES, s), lambda i: (0, i)),
                   pl.BlockSpec((SUBLANES, s), lambda i: (0, i)),
                   pl.BlockSpec((s // t, SEG_STRIDE, LANES), lambda i: (i, 0, 0))],
        out_shape=[jax.ShapeDtypeStruct((b, s, d), F32),
                   jax.ShapeDtypeStruct((SUBLANES, n), I32),
                   jax.ShapeDtypeStruct((SUBLANES, n), F32),
                   jax.ShapeDtypeStruct((n // t, SEG_STRIDE, LANES), I32)],
        scratch_shapes=[pltpu.VMEM((s + POOL_HALO, gd), F32), pltpu.VMEM((N_EXPERTS, LANES), F32)],
        compiler_params=_params(("arbitrary",)),
        name="pool_router",
    )(h3, gp2, pool_w, ps2, gf2, rt, tri)


def _dispatch_kernel(seg_ref, zb_ref, h_ref, g_ref, mrow_ref, xb_ref, buf_ref, zero_ref, sem, zsem,
                     *, n_zero):
    t = h_ref.shape[0]
    i = pl.program_id(0)
    steps = pl.num_programs(0)
    slot = lax.rem(i, 2)

    def wait_rows(n_rows, s):
        span = _rows(xb_ref, 0, n_rows)
        pltpu.make_async_copy(span, span, s).wait()

    @pl.when(i == 0)
    def _():
        zero_ref[...] = jnp.zeros(zero_ref.shape, F32)
        for e in range(N_EXPERTS):
            def zstart(r, c):
                pltpu.make_async_copy(zero_ref, _row(xb_ref, r), zsem).start()
                return c
            lax.fori_loop(zb_ref[e], zb_ref[N_EXPERTS + e], zstart, 0)

    hi = _rms(h_ref[...], g_ref[...]).astype(BF16)
    srow = lax.broadcasted_iota(I32, (2 * t, t), 0)
    perm = jnp.where((srow == mrow_ref[2:3, :]) | (srow == mrow_ref[3:4, :]), 1.0, 0.0).astype(BF16)
    xs = _dot(perm, hi)

    @pl.when(i >= 2)
    def _():
        wait_rows(2 * t, sem.at[slot])
    buf = buf_ref.at[slot]
    _rows_to_tiles(buf, xs)

    l_e = 0
    for e in range(N_EXPERTS):
        c_e = seg_ref[i * SEG_STRIDE + e]

        def start_copy(off, size, src=l_e, dst=seg_ref[i * SEG_STRIDE + N_EXPERTS + e]):
            pltpu.make_async_copy(_rows(buf, src + off, size), _rows(xb_ref, dst + off, size),
                                  sem.at[slot]).start()
        _segment_copies(c_e, start_copy, t)
        l_e = l_e + c_e

    @pl.when(i == steps - 1)
    def _():
        wait_rows(2 * t, sem.at[slot])

        @pl.when(steps > 1)
        def _():
            wait_rows(2 * t, sem.at[1 - slot])
        wait_rows(n_zero, zsem)


def _dispatch(h, g, meta, seg, zero_bounds, n_rows):
    n, d = h.shape
    t = min(ROUTER_TILE, n)
    g2 = g[None, :]
    return pl.pallas_call(
        functools.partial(_dispatch_kernel, n_zero=n_rows - 2 * n),
        grid_spec=pltpu.PrefetchScalarGridSpec(
            num_scalar_prefetch=2,
            grid=(n // t,),
            in_specs=[pl.BlockSpec((t, d), lambda i, sg, zb: (i, 0)),
                      pl.BlockSpec(g2.shape, lambda i, sg, zb: (0, 0)),
                      pl.BlockSpec((SUBLANES, t), lambda i, sg, zb: (0, i))],
            out_specs=pl.BlockSpec(memory_space=pl.ANY),
            scratch_shapes=[pltpu.VMEM((2, 2 * t * SUBLANES, LANES), F32),
                            pltpu.VMEM((SUBLANES, LANES), F32),
                            pltpu.SemaphoreType.DMA((2,)),
                            pltpu.SemaphoreType.DMA],
        ),
        out_shape=jax.ShapeDtypeStruct((n_rows * SUBLANES, LANES), F32),
        compiler_params=_params(("arbitrary",)),
        name="moe_dispatch",
    )(seg, zero_bounds, h, g2, meta)


def _expert_kernel(be_ref, x_ref, wg_ref, wu_ref, wd_ref, y_ref, wgu_s, wd_s):
    i = pl.program_id(0)
    ff = wg_ref.shape[2]

    @pl.when((i == 0) | (be_ref[i] != be_ref[jnp.maximum(i - 1, 0)]))
    def _():
        for j in range(wgu_s.shape[1] // (2 * LANES)):
            c0 = j * LANES
            w = min(LANES, ff - c0)
            for half, ref in ((0, wg_ref), (1, wu_ref)):
                s0 = (2 * j + half) * LANES
                wgu_s[:, s0:s0 + w] = ref[0, :, c0:c0 + w].astype(BF16)
                if w < LANES:
                    wgu_s[:, s0 + w:s0 + LANES] = jnp.zeros((wgu_s.shape[0], LANES - w), BF16)
        wd_s[0:ff, :] = wd_ref[0].astype(BF16)
        if wd_s.shape[0] > ff:
            wd_s[ff:, :] = jnp.zeros((wd_s.shape[0] - ff, wd_s.shape[1]), BF16)

    bm = x_ref.shape[0] // SUBLANES
    x = _tiles_to_rows(x_ref, bm).astype(BF16)
    gu = _dot(x, wgu_s[...])
    acts = []
    for c0 in range(0, gu.shape[1], 2 * LANES):
        gate = gu[:, c0:c0 + LANES]
        up = gu[:, c0 + LANES:c0 + 2 * LANES]
        acts.append((gate * jax.nn.sigmoid(gate) * up).astype(BF16))
    act = jnp.concatenate(acts, axis=-1)
    _rows_to_tiles(y_ref, _dot(act, wd_s[...]))


def _experts(xb, block_e, wg, wu, wd, a, bm):
    d, ff = wg.shape[2], wg.shape[3]
    ffp = -(-ff // LANES) * LANES
    w_spec = lambda r, c: pl.BlockSpec((None, 1, r, c), lambda i, be: (a, be[i], 0, 0))
    return pl.pallas_call(
        _expert_kernel,
        grid_spec=pltpu.PrefetchScalarGridSpec(
            num_scalar_prefetch=1,
            grid=(block_e.shape[0],),
            in_specs=[pl.BlockSpec((bm * SUBLANES, LANES), lambda i, be: (i, 0)),
                      w_spec(d, ff), w_spec(d, ff), w_spec(ff, d)],
            out_specs=pl.BlockSpec((bm * SUBLANES, LANES), lambda i, be: (i, 0)),
            scratch_shapes=[pltpu.VMEM((d, 2 * ffp), BF16), pltpu.VMEM((ffp, d), BF16)],
        ),
        out_shape=jax.ShapeDtypeStruct(xb.shape, F32),
        compiler_params=_params(("arbitrary",)),
        name="moe_experts",
    )(block_e, xb, wg, wu, wd)


def _combine_kernel(seg_ref, h_ref, mrow_ref, grow_ref, mcol_ref, fn_ref, yb_ref, out_ref,
                    buf_ref, sem, *, final_norm):
    t = h_ref.shape[0]
    i = pl.program_id(0)
    steps = pl.num_programs(0)
    slot = lax.rem(i, 2)

    def fetch(tile, sl):
        buf = buf_ref.at[sl]
        l_e = 0
        for e in range(N_EXPERTS):
            c_e = seg_ref[tile * SEG_STRIDE + e]

            def start_copy(off, size, src=seg_ref[tile * SEG_STRIDE + N_EXPERTS + e], dst=l_e):
                pltpu.make_async_copy(_rows(yb_ref, src + off, size), _rows(buf, dst + off, size),
                                      sem.at[sl]).start()
            _segment_copies(c_e, start_copy, t)
            l_e = l_e + c_e

    @pl.when(i == 0)
    def _():
        fetch(0, 0)

    @pl.when(i + 1 < steps)
    def _():
        fetch(i + 1, 1 - slot)

    buf = buf_ref.at[slot]
    pltpu.make_async_copy(_rows(yb_ref, 0, 2 * t), buf, sem.at[slot]).wait()
    ys = _tiles_to_rows(buf, 2 * t)

    ls1 = mrow_ref[2:3, :]
    ls2 = mrow_ref[3:4, :]
    srow = lax.broadcasted_iota(I32, (2 * t, t), 0)
    w = (jnp.where(srow == ls1, grow_ref[0:1, :], 0.0)
         + jnp.where(srow == ls2, grow_ref[1:2, :], 0.0))
    ysg = ys * jnp.sum(w, axis=1, keepdims=True)
    hi = ysg.astype(BF16)
    lo = (ysg - hi.astype(F32)).astype(BF16)
    scol = lax.broadcasted_iota(I32, (t, 2 * t), 1)
    unsort = jnp.where((scol == mcol_ref[:, 0:1]) | (scol == mcol_ref[:, 1:2]), 1.0, 0.0).astype(BF16)
    out = h_ref[...] + _dot(unsort, hi) + _dot(unsort, lo)
    if final_norm:
        out = _rms(out, fn_ref[...])
    out_ref[...] = out


def _combine(h, yb, seg, meta, gates, fnorm, final_norm):
    n, d = h.shape
    t = min(ROUTER_TILE, n)
    mcol = meta[2:4].T
    fn2 = fnorm[None, :]
    return pl.pallas_call(
        functools.partial(_combine_kernel, final_norm=final_norm),
        grid_spec=pltpu.PrefetchScalarGridSpec(
            num_scalar_prefetch=1,
            grid=(n // t,),
            in_specs=[pl.BlockSpec((t, d), lambda i, sg: (i, 0)),
                      pl.BlockSpec((SUBLANES, t), lambda i, sg: (0, i)),
                      pl.BlockSpec((SUBLANES, t), lambda i, sg: (0, i)),
                      pl.BlockSpec((t, 2), lambda i, sg: (i, 0)),
                      pl.BlockSpec(fn2.shape, lambda i, sg: (0, 0)),
                      pl.BlockSpec(memory_space=pl.ANY)],
            out_specs=pl.BlockSpec((t, d), lambda i, sg: (i, 0)),
            scratch_shapes=[pltpu.VMEM((2, 2 * t * SUBLANES, LANES), F32),
                            pltpu.SemaphoreType.DMA((2,))],
        ),
        out_shape=jax.ShapeDtypeStruct((n, d), F32),
        compiler_params=_params(("arbitrary",)),
        name="moe_combine",
    )(seg, h, meta, gates, mcol, fn2, yb)


def _moe(h, g, meta, gates, segv, wg, wu, wd, a, fnorm, final_norm):
    n = h.shape[0]
    bm = min(MOE_ROWS, n)
    tile_cnt = segv[:, :N_EXPERTS, 0]
    tile_base = segv[:, N_EXPERTS:, 0]
    counts = tile_cnt[-1] + tile_base[-1]
    padded = (counts + bm - 1) // bm * bm
    pad_end = jnp.cumsum(padded)
    pad_start = pad_end - padded
    n_blocks = (n * 2) // bm + N_EXPERTS
    n_rows = n_blocks * bm
    seg = jnp.concatenate([tile_cnt, tile_base + pad_start[None, :]], axis=1).reshape(-1).astype(I32)
    zero_bounds = jnp.concatenate([pad_start + counts, pad_start[1:],
                                   jnp.array([n_rows], I32)]).astype(I32)
    first_row = jnp.arange(n_blocks, dtype=I32)[:, None] * bm
    block_e = jnp.minimum(jnp.sum((pad_end[None, :] <= first_row).astype(I32), axis=1),
                          N_EXPERTS - 1).astype(I32)
    xb = _dispatch(h, g, meta, seg, zero_bounds, n_rows)
    yb = _experts(xb, block_e, wg, wu, wd, a, bm)
    return _combine(h, yb, seg, meta, gates, fnorm, final_norm)


def kernel(x, positions, attn_norm, ffn_norm, mla_w_down, mla_q_norm, mla_w_uq, mla_kv_norm,
           mla_w_ukv, mla_w_o, pool_w, pool_scale, ffn_w_gate, ffn_w_up, ffn_w_down,
           moe_router, moe_w_gate, moe_w_up, moe_w_down, final_norm):
    b, s, d = x.shape
    n = b * s
    depth = attn_norm.shape[0]
    assert depth % 2 == 0, "layers come in (attention + dense FFN, pooling + MoE) pairs"
    h = x.reshape(n, d)
    cs = _rope_table(positions)
    for layer in range(0, depth, 2):
        a = layer // 2
        q, k, v = _mla_proj(h, attn_norm[layer], mla_w_down, mla_q_norm[a], mla_w_uq,
                            mla_kv_norm[a], mla_w_ukv, a, cs, b, s)
        o = _attention(q, k, v).reshape(n, N_HEADS * V_HEAD_DIM)
        h = _oproj_ffn(h, o, ffn_norm[layer], mla_w_o, ffn_w_gate, ffn_w_up, ffn_w_down, a)
        h3, meta, gates, segv = _pool_router(h.reshape(b, s, d), attn_norm[layer + 1], pool_w[a],
                                             pool_scale[a], ffn_norm[layer + 1], moe_router[a])
        h = _moe(h3.reshape(n, d), ffn_norm[layer + 1], meta, gates, segv, moe_w_gate, moe_w_up,
                 moe_w_down, a, final_norm, layer + 2 == depth)
    return h.reshape(b, s, d)
```

```python
import functools

import jax
import jax.numpy as jnp
from jax import lax
from jax.experimental import pallas as pl
from jax.experimental.pallas import tpu as pltpu

F32 = jnp.float32
BF16 = jnp.bfloat16
I32 = jnp.int32

N_HEADS = 8
QK_NOPE_DIM = 128
QK_ROPE_DIM = 64
QK_DIM = QK_NOPE_DIM + QK_ROPE_DIM
V_HEAD_DIM = 128
V_ONES = 128
Q_LORA_RANK = 512
KV_LORA_RANK = 256
ROPE_THETA = 10000.0
POOL_WINDOWS = (2, 4, 8, 16)
POOL_HALO = 16
N_EXPERTS = 8
RMS_EPS = 1e-6
LOG2_E = 1.4426950408889634

LANES = 128
SUBLANES = 8
VMEM_LIMIT = 56 * 1024 * 1024

ROW_TILE = 512
ATTN_TILE = 512
ATTN_HEADS = 8
FF_CHUNK = 1024
WEIGHT_STAGE_ROWS = 128
ROUTER_TILE = 512
MOE_ROWS = 512


def _rms(x, g):
    ms = jnp.mean(x * x, axis=-1, keepdims=True)
    return x * lax.rsqrt(ms + RMS_EPS) * g


def _dot(a, b):
    return jnp.dot(a, b, preferred_element_type=F32)


def _dot_nt(a, b):
    return lax.dot_general(a, b, (((1,), (1,)), ((), ())), preferred_element_type=F32)


def _params(sem):
    return pltpu.CompilerParams(dimension_semantics=sem, vmem_limit_bytes=VMEM_LIMIT)


def _rows_to_tiles(ref, val):
    t = val.shape[0]
    for j in range(SUBLANES):
        ref[pl.ds(j, t, stride=SUBLANES), :] = val[:, j * LANES:(j + 1) * LANES]


def _tiles_to_rows(ref, t):
    return jnp.concatenate(
        [ref[pl.ds(j, t, stride=SUBLANES), :] for j in range(SUBLANES)], axis=-1)


def _rope_kernel(pos_ref, invf_ref, cs_ref):
    half = QK_ROPE_DIM // 2
    per_row = LANES // half
    tr = pos_ref.shape[0]
    ang = pos_ref[...].astype(F32) * invf_ref[...]
    cos = jnp.cos(ang)
    sin = jnp.sin(ang)
    for j in range(per_row):
        cj = cos[:, j * half:(j + 1) * half]
        sj = sin[:, j * half:(j + 1) * half]
        cs_ref[pl.ds(j, tr, stride=per_row), :] = jnp.concatenate([cj, cj, sj, sj], axis=1)


def _rope_table(positions):
    n = positions.size
    half = QK_ROPE_DIM // 2
    per_row = LANES // half
    inv_freq = ROPE_THETA ** (-jnp.arange(0, QK_ROPE_DIM, 2, dtype=F32) / QK_ROPE_DIM)
    pos = jnp.repeat(positions.reshape(n // per_row, per_row), half, axis=1)
    invf = jnp.tile(inv_freq, per_row)[None, :]
    rows = n // per_row
    tr = min(rows, 2048)
    return pl.pallas_call(
        _rope_kernel,
        grid=(rows // tr,),
        in_specs=[pl.BlockSpec((tr, LANES), lambda i: (i, 0)),
                  pl.BlockSpec((1, LANES), lambda i: (0, 0))],
        out_specs=pl.BlockSpec((tr * per_row, LANES), lambda i: (i, 0)),
        out_shape=jax.ShapeDtypeStruct((n, LANES), F32),
        compiler_params=_params(("arbitrary",)),
        name="rope_table",
    )(pos, invf)


def _mla_proj_kernel(h_ref, g_ref, wdn_ref, qn_ref, wq_ref, kvn_ref, wkv_ref, cs_ref,
                     q_ref, k_ref, v_ref, wd_ref, wuq_ref, wukv_ref):
    @pl.when(pl.program_id(0) == 0)
    def _():
        half = QK_ROPE_DIM // 2
        kr0 = Q_LORA_RANK + KV_LORA_RANK
        kr1 = kr0 + QK_ROPE_DIM
        wd_ref[:, 0:kr1] = wdn_ref[...].astype(BF16)
        wd_ref[:, kr1:kr1 + half] = (-wdn_ref[:, kr0 + half:kr1]).astype(BF16)
        wd_ref[:, kr1 + half:kr1 + 2 * half] = wdn_ref[:, kr0:kr0 + half].astype(BF16)
        rope0 = N_HEADS * QK_NOPE_DIM
        rot0 = rope0 + N_HEADS * QK_ROPE_DIM
        kvw = QK_NOPE_DIM + V_HEAD_DIM
        for hd in range(N_HEADS):
            b = hd * QK_DIM
            r = b + QK_NOPE_DIM
            wuq_ref[:, hd * QK_NOPE_DIM:(hd + 1) * QK_NOPE_DIM] = wq_ref[:, b:r].astype(BF16)
            wuq_ref[:, rope0 + hd * QK_ROPE_DIM:rope0 + (hd + 1) * QK_ROPE_DIM] = (
                wq_ref[:, r:r + QK_ROPE_DIM].astype(BF16))
            wuq_ref[:, rot0 + hd * QK_ROPE_DIM:rot0 + hd * QK_ROPE_DIM + half] = (
                -wq_ref[:, r + half:r + QK_ROPE_DIM]).astype(BF16)
            wuq_ref[:, rot0 + hd * QK_ROPE_DIM + half:rot0 + (hd + 1) * QK_ROPE_DIM] = (
                wq_ref[:, r:r + half].astype(BF16))
            wukv_ref[:, hd * QK_NOPE_DIM:(hd + 1) * QK_NOPE_DIM] = (
                wkv_ref[:, hd * kvw:hd * kvw + QK_NOPE_DIM].astype(BF16))
            wukv_ref[:, rope0 + hd * V_HEAD_DIM:rope0 + (hd + 1) * V_HEAD_DIM] = (
                wkv_ref[:, hd * kvw + QK_NOPE_DIM:(hd + 1) * kvw].astype(BF16))

    hn = _rms(h_ref[...], g_ref[...]).astype(BF16)
    down = _dot(hn, wd_ref[...])
    kv0 = Q_LORA_RANK
    kr0 = Q_LORA_RANK + KV_LORA_RANK
    cq = _rms(down[:, :kv0], qn_ref[...]).astype(BF16)
    ckv = _rms(down[:, kv0:kr0], kvn_ref[...]).astype(BF16)
    cs = cs_ref[...]
    cos2 = cs[:, :QK_ROPE_DIM]
    sin2 = cs[:, QK_ROPE_DIM:]
    k_rope = (down[:, kr0:kr0 + QK_ROPE_DIM] * cos2
              + down[:, kr0 + QK_ROPE_DIM:kr0 + 2 * QK_ROPE_DIM] * sin2).astype(BF16)
    q = _dot(cq, wuq_ref[...])
    kv = _dot(ckv, wukv_ref[...])
    scale = QK_DIM ** -0.5 * LOG2_E
    rope0 = N_HEADS * QK_NOPE_DIM
    rot0 = rope0 + N_HEADS * QK_ROPE_DIM
    v0 = N_HEADS * QK_NOPE_DIM
    for hd in range(N_HEADS):
        q_nope = q[:, hd * QK_NOPE_DIM:(hd + 1) * QK_NOPE_DIM] * scale
        q_rope = (q[:, rope0 + hd * QK_ROPE_DIM:rope0 + (hd + 1) * QK_ROPE_DIM] * cos2
                  + q[:, rot0 + hd * QK_ROPE_DIM:rot0 + (hd + 1) * QK_ROPE_DIM] * sin2) * scale
        q_ref[0, hd, :, 0:QK_NOPE_DIM] = q_nope.astype(BF16)
        q_ref[0, hd, :, QK_NOPE_DIM:QK_DIM] = q_rope.astype(BF16)
        k_ref[0, hd, :, 0:QK_NOPE_DIM] = kv[:, hd * QK_NOPE_DIM:(hd + 1) * QK_NOPE_DIM].astype(BF16)
        k_ref[0, hd, :, QK_NOPE_DIM:QK_DIM] = k_rope
        v_ref[0, hd, :, 0:V_HEAD_DIM] = kv[:, v0 + hd * V_HEAD_DIM:v0 + (hd + 1) * V_HEAD_DIM].astype(BF16)
        v_ref[0, hd, :, V_HEAD_DIM:] = jnp.ones((kv.shape[0], V_ONES), BF16)


def _mla_proj(h, g, w_down, q_norm, w_uq, kv_norm, w_ukv, a, cs, batch, seq):
    n, d = h.shape
    tm = min(ROW_TILE, seq)
    per_b = seq // tm
    full = lambda x: pl.BlockSpec(x.shape, lambda i: (0,) * x.ndim)
    layer = lambda x: pl.BlockSpec((None,) + x.shape[1:], lambda i: (a, 0, 0))
    head_spec = lambda w: pl.BlockSpec((1, N_HEADS, tm, w), lambda i: (i // per_b, 0, i % per_b, 0))
    g2, qn2, kvn2 = g[None, :], q_norm[None, :], kv_norm[None, :]
    kr1 = Q_LORA_RANK + KV_LORA_RANK + QK_ROPE_DIM
    return pl.pallas_call(
        _mla_proj_kernel,
        grid=(n // tm,),
        in_specs=[pl.BlockSpec((tm, d), lambda i: (i, 0)), full(g2), layer(w_down), full(qn2),
                  layer(w_uq), full(kvn2), layer(w_ukv),
                  pl.BlockSpec((tm, LANES), lambda i: (i, 0))],
        out_specs=[head_spec(QK_DIM), head_spec(QK_DIM), head_spec(V_HEAD_DIM + V_ONES)],
        out_shape=[jax.ShapeDtypeStruct((batch, N_HEADS, seq, QK_DIM), BF16),
                   jax.ShapeDtypeStruct((batch, N_HEADS, seq, QK_DIM), BF16),
                   jax.ShapeDtypeStruct((batch, N_HEADS, seq, V_HEAD_DIM + V_ONES), BF16)],
        scratch_shapes=[pltpu.VMEM((d, kr1 + QK_ROPE_DIM), BF16),
                        pltpu.VMEM((Q_LORA_RANK, N_HEADS * (QK_NOPE_DIM + 2 * QK_ROPE_DIM)), BF16),
                        pltpu.VMEM((KV_LORA_RANK, N_HEADS * (QK_NOPE_DIM + V_HEAD_DIM)), BF16)],
        compiler_params=_params(("arbitrary",)),
        name="mla_proj",
    )(h, g2, w_down, qn2, w_uq, kvn2, w_ukv, cs)


def _attn_kernel(q_ref, k_ref, v_ref, o_ref, m_ref, acc_ref):
    hb, t = q_ref.shape[1], q_ref.shape[2]
    qi = pl.program_id(2)
    m_ref[...] = jnp.full(m_ref.shape, -jnp.inf, F32)
    acc_ref[...] = jnp.zeros(acc_ref.shape, F32)

    def update(hd, r0, nr, k0, nk, causal):
        rows = slice(r0, r0 + nr)
        s = _dot_nt(q_ref[0, hd, rows, :], k_ref[0, hd, pl.ds(k0, nk), :])
        if causal:
            row = lax.broadcasted_iota(I32, s.shape, 0)
            col = lax.broadcasted_iota(I32, s.shape, 1)
            s = jnp.where(row >= col, s, -jnp.inf)
        m_prev = m_ref[hd, rows]
        m_new = jnp.maximum(m_prev, jnp.max(s, axis=1, keepdims=True))
        p = jnp.exp2(s - jnp.tile(m_new, (1, nk // LANES)))
        alpha = jnp.exp2(m_prev - m_new)
        acc_ref[hd, rows] = (jnp.tile(alpha, (1, acc_ref.shape[2] // LANES)) * acc_ref[hd, rows]
                             + _dot(p.astype(BF16), v_ref[0, hd, pl.ds(k0, nk), :]))
        m_ref[hd, rows] = m_new

    def full_tile(ki, carry):
        k0 = pl.multiple_of(ki * t, t)
        for hd in range(hb):
            update(hd, 0, t, k0, t, False)
        return carry

    lax.fori_loop(0, qi, full_tile, 0)
    d0 = pl.multiple_of(qi * t, t)
    half = t // 2
    for hd in range(hb):
        update(hd, 0, t, d0, half, True)
        update(hd, half, half, pl.multiple_of(d0 + half, half), half, True)
    for hd in range(hb):
        o_ref[0, :, hd * V_HEAD_DIM:(hd + 1) * V_HEAD_DIM] = (
            acc_ref[hd, :, 0:V_HEAD_DIM] / acc_ref[hd, :, V_HEAD_DIM:V_HEAD_DIM + LANES]
        ).astype(o_ref.dtype)


def _attention(q, k, v):
    b, nh, s, _ = q.shape
    t = min(ATTN_TILE, s)
    hb = ATTN_HEADS
    kv_spec = lambda w: pl.BlockSpec((1, hb, s, w), lambda bi, gi, qi: (bi, gi, 0, 0))
    stat = pltpu.VMEM((hb, t, LANES), F32)
    return pl.pallas_call(
        _attn_kernel,
        grid=(b, nh // hb, s // t),
        in_specs=[pl.BlockSpec((1, hb, t, QK_DIM), lambda bi, gi, qi: (bi, gi, qi, 0)),
                  kv_spec(QK_DIM), kv_spec(v.shape[3])],
        out_specs=pl.BlockSpec((1, t, hb * V_HEAD_DIM), lambda bi, gi, qi: (bi, qi, gi)),
        out_shape=jax.ShapeDtypeStruct((b, s, nh * V_HEAD_DIM), BF16),
        scratch_shapes=[stat, pltpu.VMEM((hb, t, v.shape[3]), F32)],
        compiler_params=_params(("arbitrary",) * 3),
        name="mla_attention",
    )(q, k, v)


def _load_as_bf16(src_ref, dst_ref, stage_ref, sem):
    rc = stage_ref.shape[1]
    n_chunks = src_ref.shape[0] // rc

    def copy(k):
        return pltpu.make_async_copy(src_ref.at[pl.ds(k * rc, rc)], stage_ref.at[k % 2], sem.at[k % 2])

    copy(0).start()
    for k in range(n_chunks):
        if k + 1 < n_chunks:
            copy(k + 1).start()
        copy(k).wait()
        dst_ref[k * rc:(k + 1) * rc, :] = stage_ref[k % 2].astype(BF16)


def _oproj_ffn_kernel(h_ref, o_ref, g_ref, wo_hbm, wg_hbm, wu_hbm, wd_hbm, out_ref,
                      wo_ref, wg_ref, wu_ref, wd_ref, stage_d, stage_ff, sem, *, a, chunk):
    @pl.when(pl.program_id(0) == 0)
    def _():
        _load_as_bf16(wo_hbm.at[a], wo_ref, stage_d, sem)
        _load_as_bf16(wg_hbm.at[a], wg_ref, stage_ff, sem)
        _load_as_bf16(wu_hbm.at[a], wu_ref, stage_ff, sem)
        _load_as_bf16(wd_hbm.at[a], wd_ref, stage_d, sem)

    h1 = h_ref[...] + _dot(o_ref[...], wo_ref[...])
    hn = _rms(h1, g_ref[...]).astype(BF16)
    acc = h1
    ff = wg_ref.shape[1]
    for c0 in range(0, ff, chunk):
        c1 = min(c0 + chunk, ff)
        gate = _dot(hn, wg_ref[:, c0:c1])
        up = _dot(hn, wu_ref[:, c0:c1])
        act = (gate * jax.nn.sigmoid(gate) * up).astype(BF16)
        acc = acc + _dot(act, wd_ref[c0:c1, :])
    out_ref[...] = acc


def _oproj_ffn(h, o, g, wo, wg, wu, wd, a):
    n, d = h.shape
    dv, ff = wo.shape[1], wg.shape[2]
    tm = min(ROW_TILE, n)
    g2 = g[None, :]
    hbm = pl.BlockSpec(memory_space=pl.ANY)
    return pl.pallas_call(
        functools.partial(_oproj_ffn_kernel, a=a, chunk=FF_CHUNK),
        grid=(n // tm,),
        in_specs=[pl.BlockSpec((tm, d), lambda i: (i, 0)),
                  pl.BlockSpec((tm, dv), lambda i: (i, 0)),
                  pl.BlockSpec(g2.shape, lambda i: (0, 0)), hbm, hbm, hbm, hbm],
        out_specs=pl.BlockSpec((tm, d), lambda i: (i, 0)),
        out_shape=jax.ShapeDtypeStruct((n, d), F32),
        scratch_shapes=[pltpu.VMEM((dv, d), BF16), pltpu.VMEM((d, ff), BF16),
                        pltpu.VMEM((d, ff), BF16), pltpu.VMEM((ff, d), BF16),
                        pltpu.VMEM((2, WEIGHT_STAGE_ROWS, d), F32),
                        pltpu.VMEM((2, WEIGHT_STAGE_ROWS, ff), F32),
                        pltpu.SemaphoreType.DMA((2,))],
        compiler_params=_params(("arbitrary",)),
        name="oproj_ffn",
    )(h, o, g2, wo, wg, wu, wd)


def _row(ref, r):
    return ref.at[pl.ds(pl.multiple_of(r * SUBLANES, SUBLANES), SUBLANES)]


def _rows(ref, r, n_rows):
    return ref.at[pl.ds(pl.multiple_of(r * SUBLANES, SUBLANES), n_rows * SUBLANES)]


def _segment_copies(n_rows, start_copy, max_rows):
    for b in range(max_rows.bit_length()):
        size = 1 << b

        @pl.when(((n_rows >> b) & 1) == 1)
        def _():
            start_copy(n_rows & (size - 1), size)


SEG_STRIDE = 2 * N_EXPERTS


def _pool_router_kernel(h_ref, gp_ref, pw_ref, ps_ref, gf_ref, rt_ref, tri_ref,
                        out_ref, hn_ref, meta_ref, gate_ref, seg_ref, pad_ref, carry_ref):
    s, d = h_ref.shape[1], h_ref.shape[2]
    gd = d // len(POOL_WINDOWS)
    t = tri_ref.shape[0]

    @pl.when(pl.program_id(0) == 0)
    def _():
        carry_ref[...] = jnp.zeros(carry_ref.shape, F32)

    h = h_ref[0]
    hp = _rms(h, gp_ref[...])
    t1 = lax.broadcasted_iota(I32, (s, 1), 0) + 1
    pad_ref[0:POOL_HALO, :] = jnp.zeros((POOL_HALO, gd), F32)
    for gi, w in enumerate(POOL_WINDOWS):
        cols = slice(gi * gd, (gi + 1) * gd)
        x = hp[:, cols]
        run = x
        k = 1
        while k < w:
            pad_ref[POOL_HALO:POOL_HALO + s, :] = run
            run = run + pad_ref[POOL_HALO - k:POOL_HALO - k + s, :]
            k *= 2
        cnt = jnp.minimum(t1, w).astype(F32)
        z = (run / cnt - x).astype(BF16)
        y = _dot(z, pw_ref[gi].astype(BF16)) * ps_ref[:, cols]
        out_ref[0, :, cols] = h[:, cols] + y

    for j in range(s // t):
        tok = slice(j * t, (j + 1) * t)
        hn, meta, gates, seg = _route_tile(out_ref[0, tok, :], gf_ref[...], rt_ref[...],
                                           tri_ref[...], carry_ref)
        hn_ref[0, tok, :] = hn
        meta_ref[:, tok] = meta
        gate_ref[:, tok] = gates
        seg_ref[j] = seg


def _route_tile(h, g, rt, tri, carry_ref):
    t = h.shape[0]
    hn = _rms(h, g)

    hi = hn.astype(BF16)
    lo = (hn - hi.astype(F32)).astype(BF16)
    a = _dot_nt(rt, hi)
    logits = a[:N_EXPERTS] + a[N_EXPERTS:] + _dot_nt(rt[:N_EXPERTS], lo)

    eidx = lax.broadcasted_iota(I32, logits.shape, 0)
    m1 = jnp.max(logits, axis=0, keepdims=True)
    i1 = jnp.min(jnp.where(logits == m1, eidx, N_EXPERTS), axis=0, keepdims=True)
    oh1 = eidx == i1
    rest = jnp.where(oh1, -jnp.inf, logits)
    m2 = jnp.max(rest, axis=0, keepdims=True)
    i2 = jnp.min(jnp.where(rest == m2, eidx, N_EXPERTS), axis=0, keepdims=True)
    oh2 = eidx == i2
    e2 = jnp.exp(m2 - m1)
    den = 1.0 + e2
    g1 = 1.0 / den
    g2 = e2 / den

    sel = jnp.where(oh1 | oh2, 1.0, 0.0).astype(F32)
    pref = _dot(sel.astype(BF16), tri)
    carry = carry_ref[...]
    cnt = jnp.broadcast_to(jnp.sum(sel, axis=1, keepdims=True), (N_EXPERTS, LANES))
    erow = lax.broadcasted_iota(I32, cnt.shape, 0)
    incl = cnt
    for sh in (1, 2, 4):
        incl = incl + jnp.where(erow >= sh, pltpu.roll(incl, sh, axis=0), 0.0)
    lrow = (incl - cnt)[:, 0:1] + pref
    ls1 = jnp.sum(jnp.where(oh1, lrow, 0.0), axis=0, keepdims=True).astype(I32)
    ls2 = jnp.sum(jnp.where(oh2, lrow, 0.0), axis=0, keepdims=True).astype(I32)
    carry_ref[...] = carry + cnt

    zi = jnp.zeros((SUBLANES - 4, t), I32)
    meta = jnp.concatenate([i1, i2, ls1, ls2, zi], axis=0)
    zf = jnp.zeros((SUBLANES - 2, t), F32)
    gates = jnp.concatenate([g1, g2, zf], axis=0)
    return hi, meta, gates, jnp.concatenate([cnt, carry], axis=0).astype(I32)


def _pool_router(h3, g_pool, pool_w, pool_scale, g_ffn, router):
    b, s, d = h3.shape
    n = b * s
    gd = d // len(POOL_WINDOWS)
    t = min(ROUTER_TILE, s)
    r_t = router.T
    r_hi = r_t.astype(BF16)
    r_lo = (r_t - r_hi.astype(F32)).astype(BF16)
    rt = jnp.concatenate([r_hi, r_lo], axis=0)
    tri = jnp.triu(jnp.ones((t, t), BF16), k=1)
    full = lambda a: pl.BlockSpec(a.shape, lambda i: (0,) * a.ndim)
    gp2, ps2, gf2 = g_pool[None, :], pool_scale[None, :], g_ffn[None, :]
    return pl.pallas_call(
        _pool_router_kernel,
        grid=(b,),
        in_specs=[pl.BlockSpec((1, s, d), lambda i: (i, 0, 0)), full(gp2), full(pool_w), full(ps2),
                  full(gf2), full(rt), full(tri)],
        out_specs=[pl.BlockSpec((1, s, d), lambda i: (i, 0, 0)),
                   pl.BlockSpec((1, s, d), lambda i: (i, 0, 0)),
                   pl.BlockSpec((SUBLANES, s), lambda i: (0, i)),
                   pl.BlockSpec((SUBLANES, s), lambda i: (0, i)),
                   pl.BlockSpec((s // t, SEG_STRIDE, LANES), lambda i: (i, 0, 0))],
        out_shape=[jax.ShapeDtypeStruct((b, s, d), F32),
                   jax.ShapeDtypeStruct((b, s, d), BF16),
                   jax.ShapeDtypeStruct((SUBLANES, n), I32),
                   jax.ShapeDtypeStruct((SUBLANES, n), F32),
                   jax.ShapeDtypeStruct((n // t, SEG_STRIDE, LANES), I32)],
        scratch_shapes=[pltpu.VMEM((s + POOL_HALO, gd), F32), pltpu.VMEM((N_EXPERTS, LANES), F32)],
        compiler_params=_params(("arbitrary",)),
        name="pool_router",
    )(h3, gp2, pool_w, ps2, gf2, rt, tri)


def _dispatch_kernel(seg_ref, zb_ref, hn_ref, mrow_ref, xb_ref, buf_ref, zero_ref, sem, zsem,
                     *, n_zero):
    t = hn_ref.shape[0]
    i = pl.program_id(0)
    steps = pl.num_programs(0)
    slot = lax.rem(i, 2)

    def wait_rows(n_rows, s):
        span = _rows(xb_ref, 0, n_rows)
        pltpu.make_async_copy(span, span, s).wait()

    @pl.when(i == 0)
    def _():
        zero_ref[...] = jnp.zeros(zero_ref.shape, F32)
        for e in range(N_EXPERTS):
            def zstart(r, c):
                pltpu.make_async_copy(zero_ref, _row(xb_ref, r), zsem).start()
                return c
            lax.fori_loop(zb_ref[e], zb_ref[N_EXPERTS + e], zstart, 0)

    srow = lax.broadcasted_iota(I32, (2 * t, t), 0)
    perm = jnp.where((srow == mrow_ref[2:3, :]) | (srow == mrow_ref[3:4, :]), 1.0, 0.0).astype(BF16)
    xs = _dot(perm, hn_ref[...])

    @pl.when(i >= 2)
    def _():
        wait_rows(2 * t, sem.at[slot])
    buf = buf_ref.at[slot]
    _rows_to_tiles(buf, xs)

    l_e = 0
    for e in range(N_EXPERTS):
        c_e = seg_ref[i * SEG_STRIDE + e]

        def start_copy(off, size, src=l_e, dst=seg_ref[i * SEG_STRIDE + N_EXPERTS + e]):
            pltpu.make_async_copy(_rows(buf, src + off, size), _rows(xb_ref, dst + off, size),
                                  sem.at[slot]).start()
        _segment_copies(c_e, start_copy, t)
        l_e = l_e + c_e

    @pl.when(i == steps - 1)
    def _():
        wait_rows(2 * t, sem.at[slot])

        @pl.when(steps > 1)
        def _():
            wait_rows(2 * t, sem.at[1 - slot])
        wait_rows(n_zero, zsem)


def _dispatch(hn, meta, seg, zero_bounds, n_rows):
    n, d = hn.shape
    t = min(ROUTER_TILE, n)
    return pl.pallas_call(
        functools.partial(_dispatch_kernel, n_zero=n_rows - 2 * n),
        grid_spec=pltpu.PrefetchScalarGridSpec(
            num_scalar_prefetch=2,
            grid=(n // t,),
            in_specs=[pl.BlockSpec((t, d), lambda i, sg, zb: (i, 0)),
                      pl.BlockSpec((SUBLANES, t), lambda i, sg, zb: (0, i))],
            out_specs=pl.BlockSpec(memory_space=pl.ANY),
            scratch_shapes=[pltpu.VMEM((2, 2 * t * SUBLANES, LANES), F32),
                            pltpu.VMEM((SUBLANES, LANES), F32),
                            pltpu.SemaphoreType.DMA((2,)),
                            pltpu.SemaphoreType.DMA],
        ),
        out_shape=jax.ShapeDtypeStruct((n_rows * SUBLANES, LANES), F32),
        compiler_params=_params(("arbitrary",)),
        name="moe_dispatch",
    )(seg, zero_bounds, hn, meta)


def _expert_kernel(be_ref, x_ref, wg_ref, wu_ref, wd_ref, y_ref, wgu_s, wd_s):
    i = pl.program_id(0)
    ff = wg_ref.shape[2]

    @pl.when((i == 0) | (be_ref[i] != be_ref[jnp.maximum(i - 1, 0)]))
    def _():
        for j in range(wgu_s.shape[1] // (2 * LANES)):
            c0 = j * LANES
            w = min(LANES, ff - c0)
            for half, ref in ((0, wg_ref), (1, wu_ref)):
                s0 = (2 * j + half) * LANES
                wgu_s[:, s0:s0 + w] = ref[0, :, c0:c0 + w].astype(BF16)
                if w < LANES:
                    wgu_s[:, s0 + w:s0 + LANES] = jnp.zeros((wgu_s.shape[0], LANES - w), BF16)
        wd_s[0:ff, :] = wd_ref[0].astype(BF16)
        if wd_s.shape[0] > ff:
            wd_s[ff:, :] = jnp.zeros((wd_s.shape[0] - ff, wd_s.shape[1]), BF16)

    bm = x_ref.shape[0] // SUBLANES
    x = _tiles_to_rows(x_ref, bm).astype(BF16)
    gu = _dot(x, wgu_s[...])
    acts = []
    for c0 in range(0, gu.shape[1], 2 * LANES):
        gate = gu[:, c0:c0 + LANES]
        up = gu[:, c0 + LANES:c0 + 2 * LANES]
        acts.append((gate * jax.nn.sigmoid(gate) * up).astype(BF16))
    act = jnp.concatenate(acts, axis=-1)
    _rows_to_tiles(y_ref, _dot(act, wd_s[...]))


def _experts(xb, block_e, wg, wu, wd, a, bm):
    d, ff = wg.shape[2], wg.shape[3]
    ffp = -(-ff // LANES) * LANES
    w_spec = lambda r, c: pl.BlockSpec((None, 1, r, c), lambda i, be: (a, be[i], 0, 0))
    return pl.pallas_call(
        _expert_kernel,
        grid_spec=pltpu.PrefetchScalarGridSpec(
            num_scalar_prefetch=1,
            grid=(block_e.shape[0],),
            in_specs=[pl.BlockSpec((bm * SUBLANES, LANES), lambda i, be: (i, 0)),
                      w_spec(d, ff), w_spec(d, ff), w_spec(ff, d)],
            out_specs=pl.BlockSpec((bm * SUBLANES, LANES), lambda i, be: (i, 0)),
            scratch_shapes=[pltpu.VMEM((d, 2 * ffp), BF16), pltpu.VMEM((ffp, d), BF16)],
        ),
        out_shape=jax.ShapeDtypeStruct(xb.shape, F32),
        compiler_params=_params(("arbitrary",)),
        name="moe_experts",
    )(block_e, xb, wg, wu, wd)


def _combine_kernel(seg_ref, h_ref, mrow_ref, grow_ref, mcol_ref, fn_ref, yb_ref, out_ref,
                    buf_ref, sem, *, final_norm):
    t = h_ref.shape[0]
    i = pl.program_id(0)
    steps = pl.num_programs(0)
    slot = lax.rem(i, 2)

    def fetch(tile, sl):
        buf = buf_ref.at[sl]
        l_e = 0
        for e in range(N_EXPERTS):
            c_e = seg_ref[tile * SEG_STRIDE + e]

            def start_copy(off, size, src=seg_ref[tile * SEG_STRIDE + N_EXPERTS + e], dst=l_e):
                pltpu.make_async_copy(_rows(yb_ref, src + off, size), _rows(buf, dst + off, size),
                                      sem.at[sl]).start()
            _segment_copies(c_e, start_copy, t)
            l_e = l_e + c_e

    @pl.when(i == 0)
    def _():
        fetch(0, 0)

    @pl.when(i + 1 < steps)
    def _():
        fetch(i + 1, 1 - slot)

    buf = buf_ref.at[slot]
    pltpu.make_async_copy(_rows(yb_ref, 0, 2 * t), buf, sem.at[slot]).wait()
    ys = _tiles_to_rows(buf, 2 * t)

    ls1 = mrow_ref[2:3, :]
    ls2 = mrow_ref[3:4, :]
    srow = lax.broadcasted_iota(I32, (2 * t, t), 0)
    w = (jnp.where(srow == ls1, grow_ref[0:1, :], 0.0)
         + jnp.where(srow == ls2, grow_ref[1:2, :], 0.0))
    ysg = ys * jnp.sum(w, axis=1, keepdims=True)
    hi = ysg.astype(BF16)
    lo = (ysg - hi.astype(F32)).astype(BF16)
    scol = lax.broadcasted_iota(I32, (t, 2 * t), 1)
    unsort = jnp.where((scol == mcol_ref[:, 0:1]) | (scol == mcol_ref[:, 1:2]), 1.0, 0.0).astype(BF16)
    out = h_ref[...] + _dot(unsort, hi) + _dot(unsort, lo)
    if final_norm:
        out = _rms(out, fn_ref[...])
    out_ref[...] = out


def _combine(h, yb, seg, meta, gates, fnorm, final_norm):
    n, d = h.shape
    t = min(ROUTER_TILE, n)
    mcol = meta[2:4].T
    fn2 = fnorm[None, :]
    return pl.pallas_call(
        functools.partial(_combine_kernel, final_norm=final_norm),
        grid_spec=pltpu.PrefetchScalarGridSpec(
            num_scalar_prefetch=1,
            grid=(n // t,),
            in_specs=[pl.BlockSpec((t, d), lambda i, sg: (i, 0)),
                      pl.BlockSpec((SUBLANES, t), lambda i, sg: (0, i)),
                      pl.BlockSpec((SUBLANES, t), lambda i, sg: (0, i)),
                      pl.BlockSpec((t, 2), lambda i, sg: (i, 0)),
                      pl.BlockSpec(fn2.shape, lambda i, sg: (0, 0)),
                      pl.BlockSpec(memory_space=pl.ANY)],
            out_specs=pl.BlockSpec((t, d), lambda i, sg: (i, 0)),
            scratch_shapes=[pltpu.VMEM((2, 2 * t * SUBLANES, LANES), F32),
                            pltpu.SemaphoreType.DMA((2,))],
        ),
        out_shape=jax.ShapeDtypeStruct((n, d), F32),
        compiler_params=_params(("arbitrary",)),
        name="moe_combine",
    )(seg, h, meta, gates, mcol, fn2, yb)


def _moe(h, hn, meta, gates, segv, wg, wu, wd, a, fnorm, final_norm):
    n = h.shape[0]
    bm = min(MOE_ROWS, n)
    tile_cnt = segv[:, :N_EXPERTS, 0]
    tile_base = segv[:, N_EXPERTS:, 0]
    counts = tile_cnt[-1] + tile_base[-1]
    padded = (counts + bm - 1) // bm * bm
    pad_end = jnp.cumsum(padded)
    pad_start = pad_end - padded
    n_blocks = (n * 2) // bm + N_EXPERTS
    n_rows = n_blocks * bm
    seg = jnp.concatenate([tile_cnt, tile_base + pad_start[None, :]], axis=1).reshape(-1).astype(I32)
    zero_bounds = jnp.concatenate([pad_start + counts, pad_start[1:],
                                   jnp.array([n_rows], I32)]).astype(I32)
    first_row = jnp.arange(n_blocks, dtype=I32)[:, None] * bm
    block_e = jnp.minimum(jnp.sum((pad_end[None, :] <= first_row).astype(I32), axis=1),
                          N_EXPERTS - 1).astype(I32)
    xb = _dispatch(hn, meta, seg, zero_bounds, n_rows)
    yb = _experts(xb, block_e, wg, wu, wd, a, bm)
    return _combine(h, yb, seg, meta, gates, fnorm, final_norm)


def kernel(x, positions, attn_norm, ffn_norm, mla_w_down, mla_q_norm, mla_w_uq, mla_kv_norm,
           mla_w_ukv, mla_w_o, pool_w, pool_scale, ffn_w_gate, ffn_w_up, ffn_w_down,
           moe_router, moe_w_gate, moe_w_up, moe_w_down, final_norm):
    b, s, d = x.shape
    n = b * s
    depth = attn_norm.shape[0]
    assert depth % 2 == 0, "layers come in (attention + dense FFN, pooling + MoE) pairs"
    h = x.reshape(n, d)
    cs = _rope_table(positions)
    for layer in range(0, depth, 2):
        a = layer // 2
        q, k, v = _mla_proj(h, attn_norm[layer], mla_w_down, mla_q_norm[a], mla_w_uq,
                            mla_kv_norm[a], mla_w_ukv, a, cs, b, s)
        o = _attention(q, k, v).reshape(n, N_HEADS * V_HEAD_DIM)
        h = _oproj_ffn(h, o, ffn_norm[layer], mla_w_o, ffn_w_gate, ffn_w_up, ffn_w_down, a)
        h3, hn3, meta, gates, segv = _pool_router(h.reshape(b, s, d), attn_norm[layer + 1],
                                                  pool_w[a], pool_scale[a], ffn_norm[layer + 1],
                                                  moe_router[a])
        h = _moe(h3.reshape(n, d), hn3.reshape(n, d), meta, gates, segv, moe_w_gate, moe_w_up,
                 moe_w_down, a, final_norm, layer + 2 == depth)
    return h.reshape(b, s, d)
```

```python
import functools

import jax
import jax.numpy as jnp
from jax import lax
from jax.experimental import pallas as pl
from jax.experimental.pallas import tpu as pltpu

F32 = jnp.float32
BF16 = jnp.bfloat16
I32 = jnp.int32

N_HEADS = 8
QK_NOPE_DIM = 128
QK_ROPE_DIM = 64
QK_DIM = QK_NOPE_DIM + QK_ROPE_DIM
V_HEAD_DIM = 128
V_ONES = 128
Q_LORA_RANK = 512
KV_LORA_RANK = 256
ROPE_THETA = 10000.0
POOL_WINDOWS = (2, 4, 8, 16)
POOL_HALO = 16
N_EXPERTS = 8
RMS_EPS = 1e-6
LOG2_E = 1.4426950408889634

LANES = 128
SUBLANES = 8
VMEM_LIMIT = 56 * 1024 * 1024

ROW_TILE = 512
ATTN_TILE = 512
ATTN_HEADS = 8
FF_CHUNK = 1024
WEIGHT_STAGE_ROWS = 128
ROUTER_TILE = 512
MOE_ROWS = 512


def _rms(x, g):
    ms = jnp.mean(x * x, axis=-1, keepdims=True)
    return x * lax.rsqrt(ms + RMS_EPS) * g


def _dot(a, b):
    return jnp.dot(a, b, preferred_element_type=F32)


def _dot_nt(a, b):
    return lax.dot_general(a, b, (((1,), (1,)), ((), ())), preferred_element_type=F32)


def _params(sem):
    return pltpu.CompilerParams(dimension_semantics=sem, vmem_limit_bytes=VMEM_LIMIT)


def _rows_to_tiles(ref, val):
    t = val.shape[0]
    for j in range(SUBLANES):
        ref[pl.ds(j, t, stride=SUBLANES), :] = val[:, j * LANES:(j + 1) * LANES]


def _tiles_to_rows(ref, t):
    return jnp.concatenate(
        [ref[pl.ds(j, t, stride=SUBLANES), :] for j in range(SUBLANES)], axis=-1)


def _rope_kernel(pos_ref, invf_ref, cs_ref):
    half = QK_ROPE_DIM // 2
    per_row = LANES // half
    tr = pos_ref.shape[0]
    ang = pos_ref[...].astype(F32) * invf_ref[...]
    cos = jnp.cos(ang)
    sin = jnp.sin(ang)
    for j in range(per_row):
        cj = cos[:, j * half:(j + 1) * half]
        sj = sin[:, j * half:(j + 1) * half]
        cs_ref[pl.ds(j, tr, stride=per_row), :] = jnp.concatenate([cj, cj, sj, sj], axis=1)


def _rope_table(positions):
    n = positions.size
    half = QK_ROPE_DIM // 2
    per_row = LANES // half
    inv_freq = ROPE_THETA ** (-jnp.arange(0, QK_ROPE_DIM, 2, dtype=F32) / QK_ROPE_DIM)
    pos = jnp.repeat(positions.reshape(n // per_row, per_row), half, axis=1)
    invf = jnp.tile(inv_freq, per_row)[None, :]
    rows = n // per_row
    tr = min(rows, 2048)
    return pl.pallas_call(
        _rope_kernel,
        grid=(rows // tr,),
        in_specs=[pl.BlockSpec((tr, LANES), lambda i: (i, 0)),
                  pl.BlockSpec((1, LANES), lambda i: (0, 0))],
        out_specs=pl.BlockSpec((tr * per_row, LANES), lambda i: (i, 0)),
        out_shape=jax.ShapeDtypeStruct((n, LANES), F32),
        compiler_params=_params(("arbitrary",)),
        name="rope_table",
    )(pos, invf)


def _mla_proj_kernel(h_ref, g_ref, wdn_ref, qn_ref, wq_ref, kvn_ref, wkv_ref, cs_ref,
                     q_ref, k_ref, v_ref, wd_ref, wuq_ref, wukv_ref):
    @pl.when(pl.program_id(0) == 0)
    def _():
        half = QK_ROPE_DIM // 2
        kr0 = Q_LORA_RANK + KV_LORA_RANK
        kr1 = kr0 + QK_ROPE_DIM
        wd_ref[:, 0:kr1] = wdn_ref[...].astype(BF16)
        wd_ref[:, kr1:kr1 + half] = (-wdn_ref[:, kr0 + half:kr1]).astype(BF16)
        wd_ref[:, kr1 + half:kr1 + 2 * half] = wdn_ref[:, kr0:kr0 + half].astype(BF16)
        rope0 = N_HEADS * QK_NOPE_DIM
        rot0 = rope0 + N_HEADS * QK_ROPE_DIM
        kvw = QK_NOPE_DIM + V_HEAD_DIM
        for hd in range(N_HEADS):
            b = hd * QK_DIM
            r = b + QK_NOPE_DIM
            wuq_ref[:, hd * QK_NOPE_DIM:(hd + 1) * QK_NOPE_DIM] = wq_ref[:, b:r].astype(BF16)
            wuq_ref[:, rope0 + hd * QK_ROPE_DIM:rope0 + (hd + 1) * QK_ROPE_DIM] = (
                wq_ref[:, r:r + QK_ROPE_DIM].astype(BF16))
            wuq_ref[:, rot0 + hd * QK_ROPE_DIM:rot0 + hd * QK_ROPE_DIM + half] = (
                -wq_ref[:, r + half:r + QK_ROPE_DIM]).astype(BF16)
            wuq_ref[:, rot0 + hd * QK_ROPE_DIM + half:rot0 + (hd + 1) * QK_ROPE_DIM] = (
                wq_ref[:, r:r + half].astype(BF16))
            wukv_ref[:, hd * QK_NOPE_DIM:(hd + 1) * QK_NOPE_DIM] = (
                wkv_ref[:, hd * kvw:hd * kvw + QK_NOPE_DIM].astype(BF16))
            wukv_ref[:, rope0 + hd * V_HEAD_DIM:rope0 + (hd + 1) * V_HEAD_DIM] = (
                wkv_ref[:, hd * kvw + QK_NOPE_DIM:(hd + 1) * kvw].astype(BF16))

    hn = _rms(h_ref[...], g_ref[...]).astype(BF16)
    down = _dot(hn, wd_ref[...])
    kv0 = Q_LORA_RANK
    kr0 = Q_LORA_RANK + KV_LORA_RANK
    cq = _rms(down[:, :kv0], qn_ref[...]).astype(BF16)
    ckv = _rms(down[:, kv0:kr0], kvn_ref[...]).astype(BF16)
    cs = cs_ref[...]
    cos2 = cs[:, :QK_ROPE_DIM]
    sin2 = cs[:, QK_ROPE_DIM:]
    k_rope = (down[:, kr0:kr0 + QK_ROPE_DIM] * cos2
              + down[:, kr0 + QK_ROPE_DIM:kr0 + 2 * QK_ROPE_DIM] * sin2).astype(BF16)
    q = _dot(cq, wuq_ref[...])
    kv = _dot(ckv, wukv_ref[...])
    scale = QK_DIM ** -0.5 * LOG2_E
    rope0 = N_HEADS * QK_NOPE_DIM
    rot0 = rope0 + N_HEADS * QK_ROPE_DIM
    v0 = N_HEADS * QK_NOPE_DIM
    for hd in range(N_HEADS):
        q_nope = q[:, hd * QK_NOPE_DIM:(hd + 1) * QK_NOPE_DIM] * scale
        q_rope = (q[:, rope0 + hd * QK_ROPE_DIM:rope0 + (hd + 1) * QK_ROPE_DIM] * cos2
                  + q[:, rot0 + hd * QK_ROPE_DIM:rot0 + (hd + 1) * QK_ROPE_DIM] * sin2) * scale
        q_ref[0, hd, :, 0:QK_NOPE_DIM] = q_nope.astype(BF16)
        q_ref[0, hd, :, QK_NOPE_DIM:QK_DIM] = q_rope.astype(BF16)
        k_ref[0, hd, :, 0:QK_NOPE_DIM] = kv[:, hd * QK_NOPE_DIM:(hd + 1) * QK_NOPE_DIM].astype(BF16)
        k_ref[0, hd, :, QK_NOPE_DIM:QK_DIM] = k_rope
        v_ref[0, hd, :, 0:V_HEAD_DIM] = kv[:, v0 + hd * V_HEAD_DIM:v0 + (hd + 1) * V_HEAD_DIM].astype(BF16)
        v_ref[0, hd, :, V_HEAD_DIM:] = jnp.ones((kv.shape[0], V_ONES), BF16)


def _mla_proj(h, g, w_down, q_norm, w_uq, kv_norm, w_ukv, a, cs, batch, seq):
    n, d = h.shape
    tm = min(ROW_TILE, seq)
    per_b = seq // tm
    full = lambda x: pl.BlockSpec(x.shape, lambda i: (0,) * x.ndim)
    layer = lambda x: pl.BlockSpec((None,) + x.shape[1:], lambda i: (a, 0, 0))
    head_spec = lambda w: pl.BlockSpec((1, N_HEADS, tm, w), lambda i: (i // per_b, 0, i % per_b, 0))
    g2, qn2, kvn2 = g[None, :], q_norm[None, :], kv_norm[None, :]
    kr1 = Q_LORA_RANK + KV_LORA_RANK + QK_ROPE_DIM
    return pl.pallas_call(
        _mla_proj_kernel,
        grid=(n // tm,),
        in_specs=[pl.BlockSpec((tm, d), lambda i: (i, 0)), full(g2), layer(w_down), full(qn2),
                  layer(w_uq), full(kvn2), layer(w_ukv),
                  pl.BlockSpec((tm, LANES), lambda i: (i, 0))],
        out_specs=[head_spec(QK_DIM), head_spec(QK_DIM), head_spec(V_HEAD_DIM + V_ONES)],
        out_shape=[jax.ShapeDtypeStruct((batch, N_HEADS, seq, QK_DIM), BF16),
                   jax.ShapeDtypeStruct((batch, N_HEADS, seq, QK_DIM), BF16),
                   jax.ShapeDtypeStruct((batch, N_HEADS, seq, V_HEAD_DIM + V_ONES), BF16)],
        scratch_shapes=[pltpu.VMEM((d, kr1 + QK_ROPE_DIM), BF16),
                        pltpu.VMEM((Q_LORA_RANK, N_HEADS * (QK_NOPE_DIM + 2 * QK_ROPE_DIM)), BF16),
                        pltpu.VMEM((KV_LORA_RANK, N_HEADS * (QK_NOPE_DIM + V_HEAD_DIM)), BF16)],
        compiler_params=_params(("arbitrary",)),
        name="mla_proj",
    )(h, g2, w_down, qn2, w_uq, kvn2, w_ukv, cs)


def _attn_kernel(q_ref, k_ref, v_ref, o_ref, m_ref, acc_ref):
    hb, t = q_ref.shape[1], q_ref.shape[2]
    qi = pl.program_id(2)
    m_ref[...] = jnp.full(m_ref.shape, -jnp.inf, F32)
    acc_ref[...] = jnp.zeros(acc_ref.shape, F32)

    def update(hd, r0, nr, k0, nk, causal):
        rows = slice(r0, r0 + nr)
        s = _dot_nt(q_ref[0, hd, rows, :], k_ref[0, hd, pl.ds(k0, nk), :])
        if causal:
            row = lax.broadcasted_iota(I32, s.shape, 0)
            col = lax.broadcasted_iota(I32, s.shape, 1)
            s = jnp.where(row >= col, s, -jnp.inf)
        m_prev = m_ref[hd, rows]
        m_new = jnp.maximum(m_prev, jnp.max(s, axis=1, keepdims=True))
        p = jnp.exp2(s - jnp.tile(m_new, (1, nk // LANES)))
        alpha = jnp.exp2(m_prev - m_new)
        acc_ref[hd, rows] = (jnp.tile(alpha, (1, acc_ref.shape[2] // LANES)) * acc_ref[hd, rows]
                             + _dot(p.astype(BF16), v_ref[0, hd, pl.ds(k0, nk), :]))
        m_ref[hd, rows] = m_new

    def full_tile(ki, carry):
        k0 = pl.multiple_of(ki * t, t)
        for hd in range(hb):
            update(hd, 0, t, k0, t, False)
        return carry

    lax.fori_loop(0, qi, full_tile, 0)
    d0 = pl.multiple_of(qi * t, t)
    half = t // 2
    for hd in range(hb):
        update(hd, 0, t, d0, half, True)
        update(hd, half, half, pl.multiple_of(d0 + half, half), half, True)
    for hd in range(hb):
        o_ref[0, :, hd * V_HEAD_DIM:(hd + 1) * V_HEAD_DIM] = (
            acc_ref[hd, :, 0:V_HEAD_DIM] / acc_ref[hd, :, V_HEAD_DIM:V_HEAD_DIM + LANES]
        ).astype(o_ref.dtype)


def _attention(q, k, v):
    b, nh, s, _ = q.shape
    t = min(ATTN_TILE, s)
    hb = ATTN_HEADS
    kv_spec = lambda w: pl.BlockSpec((1, hb, s, w), lambda bi, gi, qi: (bi, gi, 0, 0))
    stat = pltpu.VMEM((hb, t, LANES), F32)
    return pl.pallas_call(
        _attn_kernel,
        grid=(b, nh // hb, s // t),
        in_specs=[pl.BlockSpec((1, hb, t, QK_DIM), lambda bi, gi, qi: (bi, gi, qi, 0)),
                  kv_spec(QK_DIM), kv_spec(v.shape[3])],
        out_specs=pl.BlockSpec((1, t, hb * V_HEAD_DIM), lambda bi, gi, qi: (bi, qi, gi)),
        out_shape=jax.ShapeDtypeStruct((b, s, nh * V_HEAD_DIM), BF16),
        scratch_shapes=[stat, pltpu.VMEM((hb, t, v.shape[3]), F32)],
        compiler_params=_params(("arbitrary",) * 3),
        name="mla_attention",
    )(q, k, v)


def _load_as_bf16(src_ref, dst_ref, stage_ref, sem):
    rc = stage_ref.shape[1]
    n_chunks = src_ref.shape[0] // rc

    def copy(k):
        return pltpu.make_async_copy(src_ref.at[pl.ds(k * rc, rc)], stage_ref.at[k % 2], sem.at[k % 2])

    copy(0).start()
    for k in range(n_chunks):
        if k + 1 < n_chunks:
            copy(k + 1).start()
        copy(k).wait()
        dst_ref[k * rc:(k + 1) * rc, :] = stage_ref[k % 2].astype(BF16)


def _oproj_ffn_kernel(h_ref, o_ref, g_ref, wo_hbm, wg_hbm, wu_hbm, wd_hbm, out_ref,
                      wo_ref, wg_ref, wu_ref, wd_ref, stage_d, stage_ff, sem, *, a, chunk):
    @pl.when(pl.program_id(0) == 0)
    def _():
        _load_as_bf16(wo_hbm.at[a], wo_ref, stage_d, sem)
        _load_as_bf16(wg_hbm.at[a], wg_ref, stage_ff, sem)
        _load_as_bf16(wu_hbm.at[a], wu_ref, stage_ff, sem)
        _load_as_bf16(wd_hbm.at[a], wd_ref, stage_d, sem)

    h1 = h_ref[...] + _dot(o_ref[...], wo_ref[...])
    hn = _rms(h1, g_ref[...]).astype(BF16)
    acc = h1
    ff = wg_ref.shape[1]
    for c0 in range(0, ff, chunk):
        c1 = min(c0 + chunk, ff)
        gate = _dot(hn, wg_ref[:, c0:c1])
        up = _dot(hn, wu_ref[:, c0:c1])
        act = (gate * jax.nn.sigmoid(gate) * up).astype(BF16)
        acc = acc + _dot(act, wd_ref[c0:c1, :])
    out_ref[...] = acc


def _oproj_ffn(h, o, g, wo, wg, wu, wd, a):
    n, d = h.shape
    dv, ff = wo.shape[1], wg.shape[2]
    tm = min(ROW_TILE, n)
    g2 = g[None, :]
    hbm = pl.BlockSpec(memory_space=pl.ANY)
    return pl.pallas_call(
        functools.partial(_oproj_ffn_kernel, a=a, chunk=FF_CHUNK),
        grid=(n // tm,),
        in_specs=[pl.BlockSpec((tm, d), lambda i: (i, 0)),
                  pl.BlockSpec((tm, dv), lambda i: (i, 0)),
                  pl.BlockSpec(g2.shape, lambda i: (0, 0)), hbm, hbm, hbm, hbm],
        out_specs=pl.BlockSpec((tm, d), lambda i: (i, 0)),
        out_shape=jax.ShapeDtypeStruct((n, d), F32),
        scratch_shapes=[pltpu.VMEM((dv, d), BF16), pltpu.VMEM((d, ff), BF16),
                        pltpu.VMEM((d, ff), BF16), pltpu.VMEM((ff, d), BF16),
                        pltpu.VMEM((2, WEIGHT_STAGE_ROWS, d), F32),
                        pltpu.VMEM((2, WEIGHT_STAGE_ROWS, ff), F32),
                        pltpu.SemaphoreType.DMA((2,))],
        compiler_params=_params(("arbitrary",)),
        name="oproj_ffn",
    )(h, o, g2, wo, wg, wu, wd)


def _row(ref, r):
    return ref.at[pl.ds(pl.multiple_of(r * SUBLANES, SUBLANES), SUBLANES)]


def _rows(ref, r, n_rows):
    return ref.at[pl.ds(pl.multiple_of(r * SUBLANES, SUBLANES), n_rows * SUBLANES)]


def _segment_copies(n_rows, start_copy, max_rows):
    for b in range(max_rows.bit_length()):
        size = 1 << b

        @pl.when(((n_rows >> b) & 1) == 1)
        def _():
            start_copy(n_rows & (size - 1), size)


SEG_STRIDE = 2 * N_EXPERTS


def _pool_router_kernel(h_ref, gp_ref, pw_ref, ps_ref, gf_ref, rt_ref, tri_ref,
                        out_ref, hn_ref, meta_ref, gate_ref, seg_ref, pad_ref, carry_ref):
    s, d = h_ref.shape[1], h_ref.shape[2]
    gd = d // len(POOL_WINDOWS)
    t = tri_ref.shape[0]

    @pl.when(pl.program_id(0) == 0)
    def _():
        carry_ref[...] = jnp.zeros(carry_ref.shape, F32)

    h = h_ref[0]
    hp = _rms(h, gp_ref[...])
    t1 = lax.broadcasted_iota(I32, (s, 1), 0) + 1
    pad_ref[0:POOL_HALO, :] = jnp.zeros((POOL_HALO, gd), F32)
    for gi, w in enumerate(POOL_WINDOWS):
        cols = slice(gi * gd, (gi + 1) * gd)
        x = hp[:, cols]
        run = x
        k = 1
        while k < w:
            pad_ref[POOL_HALO:POOL_HALO + s, :] = run
            run = run + pad_ref[POOL_HALO - k:POOL_HALO - k + s, :]
            k *= 2
        cnt = jnp.minimum(t1, w).astype(F32)
        z = (run / cnt - x).astype(BF16)
        y = _dot(z, pw_ref[gi].astype(BF16)) * ps_ref[:, cols]
        out_ref[0, :, cols] = h[:, cols] + y

    for j in range(s // t):
        tok = slice(j * t, (j + 1) * t)
        hn, meta, gates, seg = _route_tile(out_ref[0, tok, :], gf_ref[...], rt_ref[...],
                                           tri_ref[...], carry_ref)
        hn_ref[0, tok, :] = hn
        meta_ref[:, tok] = meta
        gate_ref[:, tok] = gates
        seg_ref[j] = seg


def _route_tile(h, g, rt, tri, carry_ref):
    t = h.shape[0]
    hn = _rms(h, g)

    hi = hn.astype(BF16)
    lo = (hn - hi.astype(F32)).astype(BF16)
    a = _dot_nt(rt, hi)
    logits = a[:N_EXPERTS] + a[N_EXPERTS:] + _dot_nt(rt[:N_EXPERTS], lo)

    eidx = lax.broadcasted_iota(I32, logits.shape, 0)
    m1 = jnp.max(logits, axis=0, keepdims=True)
    i1 = jnp.min(jnp.where(logits == m1, eidx, N_EXPERTS), axis=0, keepdims=True)
    oh1 = eidx == i1
    rest = jnp.where(oh1, -jnp.inf, logits)
    m2 = jnp.max(rest, axis=0, keepdims=True)
    i2 = jnp.min(jnp.where(rest == m2, eidx, N_EXPERTS), axis=0, keepdims=True)
    oh2 = eidx == i2
    e2 = jnp.exp(m2 - m1)
    den = 1.0 + e2
    g1 = 1.0 / den
    g2 = e2 / den

    sel = jnp.where(oh1 | oh2, 1.0, 0.0).astype(F32)
    pref = _dot(sel.astype(BF16), tri)
    carry = carry_ref[...]
    cnt = jnp.broadcast_to(jnp.sum(sel, axis=1, keepdims=True), (N_EXPERTS, LANES))
    erow = lax.broadcasted_iota(I32, cnt.shape, 0)
    incl = cnt
    for sh in (1, 2, 4):
        incl = incl + jnp.where(erow >= sh, pltpu.roll(incl, sh, axis=0), 0.0)
    lrow = (incl - cnt)[:, 0:1] + pref
    ls1 = jnp.sum(jnp.where(oh1, lrow, 0.0), axis=0, keepdims=True).astype(I32)
    ls2 = jnp.sum(jnp.where(oh2, lrow, 0.0), axis=0, keepdims=True).astype(I32)
    carry_ref[...] = carry + cnt

    zi = jnp.zeros((SUBLANES - 4, t), I32)
    meta = jnp.concatenate([i1, i2, ls1, ls2, zi], axis=0)
    zf = jnp.zeros((SUBLANES - 2, t), F32)
    gates = jnp.concatenate([g1, g2, zf], axis=0)
    return hi, meta, gates, jnp.concatenate([cnt, carry], axis=0).astype(I32)


def _pool_router(h3, g_pool, pool_w, pool_scale, g_ffn, router):
    b, s, d = h3.shape
    n = b * s
    gd = d // len(POOL_WINDOWS)
    t = min(ROUTER_TILE, s)
    r_t = router.T
    r_hi = r_t.astype(BF16)
    r_lo = (r_t - r_hi.astype(F32)).astype(BF16)
    rt = jnp.concatenate([r_hi, r_lo], axis=0)
    tri = jnp.triu(jnp.ones((t, t), BF16), k=1)
    full = lambda a: pl.BlockSpec(a.shape, lambda i: (0,) * a.ndim)
    gp2, ps2, gf2 = g_pool[None, :], pool_scale[None, :], g_ffn[None, :]
    return pl.pallas_call(
        _pool_router_kernel,
        grid=(b,),
        in_specs=[pl.BlockSpec((1, s, d), lambda i: (i, 0, 0)), full(gp2), full(pool_w), full(ps2),
                  full(gf2), full(rt), full(tri)],
        out_specs=[pl.BlockSpec((1, s, d), lambda i: (i, 0, 0)),
                   pl.BlockSpec((1, s, d), lambda i: (i, 0, 0)),
                   pl.BlockSpec((SUBLANES, s), lambda i: (0, i)),
                   pl.BlockSpec((SUBLANES, s), lambda i: (0, i)),
                   pl.BlockSpec((s // t, SEG_STRIDE, LANES), lambda i: (i, 0, 0))],
        out_shape=[jax.ShapeDtypeStruct((b, s, d), F32),
                   jax.ShapeDtypeStruct((b, s, d), BF16),
                   jax.ShapeDtypeStruct((SUBLANES, n), I32),
                   jax.ShapeDtypeStruct((SUBLANES, n), F32),
                   jax.ShapeDtypeStruct((n // t, SEG_STRIDE, LANES), I32)],
        scratch_shapes=[pltpu.VMEM((s + POOL_HALO, gd), F32), pltpu.VMEM((N_EXPERTS, LANES), F32)],
        compiler_params=_params(("arbitrary",)),
        name="pool_router",
    )(h3, gp2, pool_w, ps2, gf2, rt, tri)


def _dispatch_kernel(seg_ref, zb_ref, hn_ref, mrow_ref, xb_ref, buf_ref, zero_ref, sem, zsem,
                     *, n_zero):
    t = hn_ref.shape[0]
    i = pl.program_id(0)
    steps = pl.num_programs(0)
    slot = lax.rem(i, 2)

    def wait_rows(n_rows, s):
        span = _rows(xb_ref, 0, n_rows)
        pltpu.make_async_copy(span, span, s).wait()

    @pl.when(i == 0)
    def _():
        zero_ref[...] = jnp.zeros(zero_ref.shape, F32)
        for e in range(N_EXPERTS):
            def zstart(r, c):
                pltpu.make_async_copy(zero_ref, _row(xb_ref, r), zsem).start()
                return c
            lax.fori_loop(zb_ref[e], zb_ref[N_EXPERTS + e], zstart, 0)

    srow = lax.broadcasted_iota(I32, (2 * t, t), 0)
    perm = jnp.where((srow == mrow_ref[2:3, :]) | (srow == mrow_ref[3:4, :]), 1.0, 0.0).astype(BF16)
    xs = _dot(perm, hn_ref[...])

    @pl.when(i >= 2)
    def _():
        wait_rows(2 * t, sem.at[slot])
    buf = buf_ref.at[slot]
    _rows_to_tiles(buf, xs)

    l_e = 0
    for e in range(N_EXPERTS):
        c_e = seg_ref[i * SEG_STRIDE + e]

        def start_copy(off, size, src=l_e, dst=seg_ref[i * SEG_STRIDE + N_EXPERTS + e]):
            pltpu.make_async_copy(_rows(buf, src + off, size), _rows(xb_ref, dst + off, size),
                                  sem.at[slot]).start()
        _segment_copies(c_e, start_copy, t)
        l_e = l_e + c_e

    @pl.when(i == steps - 1)
    def _():
        wait_rows(2 * t, sem.at[slot])

        @pl.when(steps > 1)
        def _():
            wait_rows(2 * t, sem.at[1 - slot])
        wait_rows(n_zero, zsem)


def _dispatch(hn, meta, seg, zero_bounds, n_rows):
    n, d = hn.shape
    t = min(ROUTER_TILE, n)
    return pl.pallas_call(
        functools.partial(_dispatch_kernel, n_zero=n_rows - 2 * n),
        grid_spec=pltpu.PrefetchScalarGridSpec(
            num_scalar_prefetch=2,
            grid=(n // t,),
            in_specs=[pl.BlockSpec((t, d), lambda i, sg, zb: (i, 0)),
                      pl.BlockSpec((SUBLANES, t), lambda i, sg, zb: (0, i))],
            out_specs=pl.BlockSpec(memory_space=pl.ANY),
            scratch_shapes=[pltpu.VMEM((2, 2 * t * SUBLANES, LANES), F32),
                            pltpu.VMEM((SUBLANES, LANES), F32),
                            pltpu.SemaphoreType.DMA((2,)),
                            pltpu.SemaphoreType.DMA],
        ),
        out_shape=jax.ShapeDtypeStruct((n_rows * SUBLANES, LANES), F32),
        compiler_params=_params(("arbitrary",)),
        name="moe_dispatch",
    )(seg, zero_bounds, hn, meta)


def _expert_kernel(be_ref, x_ref, wgt_ref, wut_ref, wd_ref, y_ref, wgu_s, wd_s):
    i = pl.program_id(0)
    ff = wgt_ref.shape[1]

    @pl.when((i == 0) | (be_ref[i] != be_ref[jnp.maximum(i - 1, 0)]))
    def _():
        for j in range(wgu_s.shape[1] // (2 * LANES)):
            c0 = j * LANES
            w = min(LANES, ff - c0)
            r0 = c0 + w - LANES
            for half, ref in ((0, wgt_ref), (1, wut_ref)):
                s0 = (2 * j + half) * LANES
                cols = ref[0, r0:r0 + LANES, :].T
                wgu_s[:, s0:s0 + w] = cols[:, LANES - w:].astype(BF16)
                if w < LANES:
                    wgu_s[:, s0 + w:s0 + LANES] = jnp.zeros((wgu_s.shape[0], LANES - w), BF16)
        wd_s[0:ff, :] = wd_ref[0].astype(BF16)
        if wd_s.shape[0] > ff:
            wd_s[ff:, :] = jnp.zeros((wd_s.shape[0] - ff, wd_s.shape[1]), BF16)

    bm = x_ref.shape[0] // SUBLANES
    x = _tiles_to_rows(x_ref, bm).astype(BF16)
    gu = _dot(x, wgu_s[...])
    acts = []
    for c0 in range(0, gu.shape[1], 2 * LANES):
        gate = gu[:, c0:c0 + LANES]
        up = gu[:, c0 + LANES:c0 + 2 * LANES]
        acts.append((gate * jax.nn.sigmoid(gate) * up).astype(BF16))
    act = jnp.concatenate(acts, axis=-1)
    _rows_to_tiles(y_ref, _dot(act, wd_s[...]))


def _experts(xb, block_e, wg, wu, wd, a, bm):
    d, ff = wg.shape[2], wg.shape[3]
    ffp = -(-ff // LANES) * LANES
    w_spec = lambda r, c: pl.BlockSpec((None, 1, r, c), lambda i, be: (a, be[i], 0, 0))
    wg = jnp.swapaxes(wg, 2, 3)
    wu = jnp.swapaxes(wu, 2, 3)
    return pl.pallas_call(
        _expert_kernel,
        grid_spec=pltpu.PrefetchScalarGridSpec(
            num_scalar_prefetch=1,
            grid=(block_e.shape[0],),
            in_specs=[pl.BlockSpec((bm * SUBLANES, LANES), lambda i, be: (i, 0)),
                      w_spec(ff, d), w_spec(ff, d), w_spec(ff, d)],
            out_specs=pl.BlockSpec((bm * SUBLANES, LANES), lambda i, be: (i, 0)),
            scratch_shapes=[pltpu.VMEM((d, 2 * ffp), BF16), pltpu.VMEM((ffp, d), BF16)],
        ),
        out_shape=jax.ShapeDtypeStruct(xb.shape, F32),
        compiler_params=_params(("arbitrary",)),
        name="moe_experts",
    )(block_e, xb, wg, wu, wd)


def _combine_kernel(seg_ref, h_ref, mrow_ref, grow_ref, mcol_ref, fn_ref, yb_ref, out_ref,
                    buf_ref, sem, *, final_norm):
    t = h_ref.shape[0]
    i = pl.program_id(0)
    steps = pl.num_programs(0)
    slot = lax.rem(i, 2)

    def fetch(tile, sl):
        buf = buf_ref.at[sl]
        l_e = 0
        for e in range(N_EXPERTS):
            c_e = seg_ref[tile * SEG_STRIDE + e]

            def start_copy(off, size, src=seg_ref[tile * SEG_STRIDE + N_EXPERTS + e], dst=l_e):
                pltpu.make_async_copy(_rows(yb_ref, src + off, size), _rows(buf, dst + off, size),
                                      sem.at[sl]).start()
            _segment_copies(c_e, start_copy, t)
            l_e = l_e + c_e

    @pl.when(i == 0)
    def _():
        fetch(0, 0)

    @pl.when(i + 1 < steps)
    def _():
        fetch(i + 1, 1 - slot)

    buf = buf_ref.at[slot]
    pltpu.make_async_copy(_rows(yb_ref, 0, 2 * t), buf, sem.at[slot]).wait()
    ys = _tiles_to_rows(buf, 2 * t)

    ls1 = mrow_ref[2:3, :]
    ls2 = mrow_ref[3:4, :]
    srow = lax.broadcasted_iota(I32, (2 * t, t), 0)
    w = (jnp.where(srow == ls1, grow_ref[0:1, :], 0.0)
         + jnp.where(srow == ls2, grow_ref[1:2, :], 0.0))
    ysg = ys * jnp.sum(w, axis=1, keepdims=True)
    hi = ysg.astype(BF16)
    lo = (ysg - hi.astype(F32)).astype(BF16)
    scol = lax.broadcasted_iota(I32, (t, 2 * t), 1)
    unsort = jnp.where((scol == mcol_ref[:, 0:1]) | (scol == mcol_ref[:, 1:2]), 1.0, 0.0).astype(BF16)
    out = h_ref[...] + _dot(unsort, hi) + _dot(unsort, lo)
    if final_norm:
        out = _rms(out, fn_ref[...])
    out_ref[...] = out


def _combine(h, yb, seg, meta, gates, fnorm, final_norm):
    n, d = h.shape
    t = min(ROUTER_TILE, n)
    mcol = meta[2:4].T
    fn2 = fnorm[None, :]
    return pl.pallas_call(
        functools.partial(_combine_kernel, final_norm=final_norm),
        grid_spec=pltpu.PrefetchScalarGridSpec(
            num_scalar_prefetch=1,
            grid=(n // t,),
            in_specs=[pl.BlockSpec((t, d), lambda i, sg: (i, 0)),
                      pl.BlockSpec((SUBLANES, t), lambda i, sg: (0, i)),
                      pl.BlockSpec((SUBLANES, t), lambda i, sg: (0, i)),
                      pl.BlockSpec((t, 2), lambda i, sg: (i, 0)),
                      pl.BlockSpec(fn2.shape, lambda i, sg: (0, 0)),
                      pl.BlockSpec(memory_space=pl.ANY)],
            out_specs=pl.BlockSpec((t, d), lambda i, sg: (i, 0)),
            scratch_shapes=[pltpu.VMEM((2, 2 * t * SUBLANES, LANES), F32),
                            pltpu.SemaphoreType.DMA((2,))],
        ),
        out_shape=jax.ShapeDtypeStruct((n, d), F32),
        compiler_params=_params(("arbitrary",)),
        name="moe_combine",
    )(seg, h, meta, gates, mcol, fn2, yb)


def _moe(h, hn, meta, gates, segv, wg, wu, wd, a, fnorm, final_norm):
    n = h.shape[0]
    bm = min(MOE_ROWS, n)
    tile_cnt = segv[:, :N_EXPERTS, 0]
    tile_base = segv[:, N_EXPERTS:, 0]
    counts = tile_cnt[-1] + tile_base[-1]
    padded = (counts + bm - 1) // bm * bm
    pad_end = jnp.cumsum(padded)
    pad_start = pad_end - padded
    n_blocks = (n * 2) // bm + N_EXPERTS
    n_rows = n_blocks * bm
    seg = jnp.concatenate([tile_cnt, tile_base + pad_start[None, :]], axis=1).reshape(-1).astype(I32)
    zero_bounds = jnp.concatenate([pad_start + counts, pad_start[1:],
                                   jnp.array([n_rows], I32)]).astype(I32)
    first_row = jnp.arange(n_blocks, dtype=I32)[:, None] * bm
    block_e = jnp.minimum(jnp.sum((pad_end[None, :] <= first_row).astype(I32), axis=1),
                          N_EXPERTS - 1).astype(I32)
    xb = _dispatch(hn, meta, seg, zero_bounds, n_rows)
    yb = _experts(xb, block_e, wg, wu, wd, a, bm)
    return _combine(h, yb, seg, meta, gates, fnorm, final_norm)


def kernel(x, positions, attn_norm, ffn_norm, mla_w_down, mla_q_norm, mla_w_uq, mla_kv_norm,
           mla_w_ukv, mla_w_o, pool_w, pool_scale, ffn_w_gate, ffn_w_up, ffn_w_down,
           moe_router, moe_w_gate, moe_w_up, moe_w_down, final_norm):
    b, s, d = x.shape
    n = b * s
    depth = attn_norm.shape[0]
    assert depth % 2 == 0, "layers come in (attention + dense FFN, pooling + MoE) pairs"
    h = x.reshape(n, d)
    cs = _rope_table(positions)
    for layer in range(0, depth, 2):
        a = layer // 2
        q, k, v = _mla_proj(h, attn_norm[layer], mla_w_down, mla_q_norm[a], mla_w_uq,
                            mla_kv_norm[a], mla_w_ukv, a, cs, b, s)
        o = _attention(q, k, v).reshape(n, N_HEADS * V_HEAD_DIM)
        h = _oproj_ffn(h, o, ffn_norm[layer], mla_w_o, ffn_w_gate, ffn_w_up, ffn_w_down, a)
        h3, hn3, meta, gates, segv = _pool_router(h.reshape(b, s, d), attn_norm[layer + 1],
                                                  pool_w[a], pool_scale[a], ffn_norm[layer + 1],
                                                  moe_router[a])
        h = _moe(h3.reshape(n, d), hn3.reshape(n, d), meta, gates, segv, moe_w_gate, moe_w_up,
                 moe_w_down, a, final_norm, layer + 2 == depth)
    return h.reshape(b, s, d)
```

```python
import functools

import jax
import jax.numpy as jnp
from jax import lax
from jax.experimental import pallas as pl
from jax.experimental.pallas import tpu as pltpu

F32 = jnp.float32
BF16 = jnp.bfloat16
I32 = jnp.int32

N_HEADS = 8
QK_NOPE_DIM = 128
QK_ROPE_DIM = 64
QK_DIM = QK_NOPE_DIM + QK_ROPE_DIM
V_HEAD_DIM = 128
V_ONES = 128
Q_LORA_RANK = 512
KV_LORA_RANK = 256
ROPE_THETA = 10000.0
POOL_WINDOWS = (2, 4, 8, 16)
POOL_HALO = 16
N_EXPERTS = 8
RMS_EPS = 1e-6
LOG2_E = 1.4426950408889634

LANES = 128
SUBLANES = 8
VMEM_LIMIT = 56 * 1024 * 1024

ROW_TILE = 512
ATTN_TILE = 512
ATTN_HEADS = 8
FF_CHUNK = 1024
WEIGHT_STAGE_ROWS = 128
ROUTER_TILE = 512
MOE_ROWS = 512


def _rms(x, g):
    ms = jnp.mean(x * x, axis=-1, keepdims=True)
    return x * lax.rsqrt(ms + RMS_EPS) * g


def _dot(a, b):
    return jnp.dot(a, b, preferred_element_type=F32)


def _dot_nt(a, b):
    return lax.dot_general(a, b, (((1,), (1,)), ((), ())), preferred_element_type=F32)


def _params(sem):
    return pltpu.CompilerParams(dimension_semantics=sem, vmem_limit_bytes=VMEM_LIMIT)


def _rows_to_tiles(ref, val):
    t = val.shape[0]
    for j in range(SUBLANES):
        ref[pl.ds(j, t, stride=SUBLANES), :] = val[:, j * LANES:(j + 1) * LANES]


def _tiles_to_rows(ref, t):
    return jnp.concatenate(
        [ref[pl.ds(j, t, stride=SUBLANES), :] for j in range(SUBLANES)], axis=-1)


def _rope_kernel(pos_ref, invf_ref, cs_ref):
    half = QK_ROPE_DIM // 2
    per_row = LANES // half
    tr = pos_ref.shape[0]
    ang = pos_ref[...].astype(F32) * invf_ref[...]
    cos = jnp.cos(ang)
    sin = jnp.sin(ang)
    for j in range(per_row):
        cj = cos[:, j * half:(j + 1) * half]
        sj = sin[:, j * half:(j + 1) * half]
        cs_ref[pl.ds(j, tr, stride=per_row), :] = jnp.concatenate([cj, cj, sj, sj], axis=1)


def _rope_table(positions):
    n = positions.size
    half = QK_ROPE_DIM // 2
    per_row = LANES // half
    inv_freq = ROPE_THETA ** (-jnp.arange(0, QK_ROPE_DIM, 2, dtype=F32) / QK_ROPE_DIM)
    pos = jnp.repeat(positions.reshape(n // per_row, per_row), half, axis=1)
    invf = jnp.tile(inv_freq, per_row)[None, :]
    rows = n // per_row
    tr = min(rows, 2048)
    return pl.pallas_call(
        _rope_kernel,
        grid=(rows // tr,),
        in_specs=[pl.BlockSpec((tr, LANES), lambda i: (i, 0)),
                  pl.BlockSpec((1, LANES), lambda i: (0, 0))],
        out_specs=pl.BlockSpec((tr * per_row, LANES), lambda i: (i, 0)),
        out_shape=jax.ShapeDtypeStruct((n, LANES), F32),
        compiler_params=_params(("arbitrary",)),
        name="rope_table",
    )(pos, invf)


def _mla_proj_kernel(h_ref, g_ref, wdn_ref, qn_ref, wq_ref, kvn_ref, wkv_ref, cs_ref,
                     q_ref, k_ref, v_ref, wd_ref, wuq_ref, wukv_ref):
    @pl.when(pl.program_id(0) == 0)
    def _():
        half = QK_ROPE_DIM // 2
        kr0 = Q_LORA_RANK + KV_LORA_RANK
        kr1 = kr0 + QK_ROPE_DIM
        wd_ref[:, 0:kr1] = wdn_ref[...].astype(BF16)
        wd_ref[:, kr1:kr1 + half] = (-wdn_ref[:, kr0 + half:kr1]).astype(BF16)
        wd_ref[:, kr1 + half:kr1 + 2 * half] = wdn_ref[:, kr0:kr0 + half].astype(BF16)
        rope0 = N_HEADS * QK_NOPE_DIM
        rot0 = rope0 + N_HEADS * QK_ROPE_DIM
        kvw = QK_NOPE_DIM + V_HEAD_DIM
        for hd in range(N_HEADS):
            b = hd * QK_DIM
            r = b + QK_NOPE_DIM
            wuq_ref[:, hd * QK_NOPE_DIM:(hd + 1) * QK_NOPE_DIM] = wq_ref[:, b:r].astype(BF16)
            wuq_ref[:, rope0 + hd * QK_ROPE_DIM:rope0 + (hd + 1) * QK_ROPE_DIM] = (
                wq_ref[:, r:r + QK_ROPE_DIM].astype(BF16))
            wuq_ref[:, rot0 + hd * QK_ROPE_DIM:rot0 + hd * QK_ROPE_DIM + half] = (
                -wq_ref[:, r + half:r + QK_ROPE_DIM]).astype(BF16)
            wuq_ref[:, rot0 + hd * QK_ROPE_DIM + half:rot0 + (hd + 1) * QK_ROPE_DIM] = (
                wq_ref[:, r:r + half].astype(BF16))
            wukv_ref[:, hd * QK_NOPE_DIM:(hd + 1) * QK_NOPE_DIM] = (
                wkv_ref[:, hd * kvw:hd * kvw + QK_NOPE_DIM].astype(BF16))
            wukv_ref[:, rope0 + hd * V_HEAD_DIM:rope0 + (hd + 1) * V_HEAD_DIM] = (
                wkv_ref[:, hd * kvw + QK_NOPE_DIM:(hd + 1) * kvw].astype(BF16))

    hn = _rms(h_ref[...], g_ref[...]).astype(BF16)
    down = _dot(hn, wd_ref[...])
    kv0 = Q_LORA_RANK
    kr0 = Q_LORA_RANK + KV_LORA_RANK
    cq = _rms(down[:, :kv0], qn_ref[...]).astype(BF16)
    ckv = _rms(down[:, kv0:kr0], kvn_ref[...]).astype(BF16)
    cs = cs_ref[...]
    cos2 = cs[:, :QK_ROPE_DIM]
    sin2 = cs[:, QK_ROPE_DIM:]
    k_rope = (down[:, kr0:kr0 + QK_ROPE_DIM] * cos2
              + down[:, kr0 + QK_ROPE_DIM:kr0 + 2 * QK_ROPE_DIM] * sin2).astype(BF16)
    q = _dot(cq, wuq_ref[...])
    kv = _dot(ckv, wukv_ref[...])
    scale = QK_DIM ** -0.5 * LOG2_E
    rope0 = N_HEADS * QK_NOPE_DIM
    rot0 = rope0 + N_HEADS * QK_ROPE_DIM
    v0 = N_HEADS * QK_NOPE_DIM
    for hd in range(N_HEADS):
        q_nope = q[:, hd * QK_NOPE_DIM:(hd + 1) * QK_NOPE_DIM] * scale
        q_rope = (q[:, rope0 + hd * QK_ROPE_DIM:rope0 + (hd + 1) * QK_ROPE_DIM] * cos2
                  + q[:, rot0 + hd * QK_ROPE_DIM:rot0 + (hd + 1) * QK_ROPE_DIM] * sin2) * scale
        q_ref[0, hd, :, 0:QK_NOPE_DIM] = q_nope.astype(BF16)
        q_ref[0, hd, :, QK_NOPE_DIM:QK_DIM] = q_rope.astype(BF16)
        k_ref[0, hd, :, 0:QK_NOPE_DIM] = kv[:, hd * QK_NOPE_DIM:(hd + 1) * QK_NOPE_DIM].astype(BF16)
        k_ref[0, hd, :, QK_NOPE_DIM:QK_DIM] = k_rope
        v_ref[0, hd, :, 0:V_HEAD_DIM] = kv[:, v0 + hd * V_HEAD_DIM:v0 + (hd + 1) * V_HEAD_DIM].astype(BF16)
        v_ref[0, hd, :, V_HEAD_DIM:] = jnp.ones((kv.shape[0], V_ONES), BF16)


def _mla_proj(h, g, w_down, q_norm, w_uq, kv_norm, w_ukv, a, cs, batch, seq):
    n, d = h.shape
    tm = min(ROW_TILE, seq)
    per_b = seq // tm
    full = lambda x: pl.BlockSpec(x.shape, lambda i: (0,) * x.ndim)
    layer = lambda x: pl.BlockSpec((None,) + x.shape[1:], lambda i: (a, 0, 0))
    head_spec = lambda w: pl.BlockSpec((1, N_HEADS, tm, w), lambda i: (i // per_b, 0, i % per_b, 0))
    g2, qn2, kvn2 = g[None, :], q_norm[None, :], kv_norm[None, :]
    kr1 = Q_LORA_RANK + KV_LORA_RANK + QK_ROPE_DIM
    return pl.pallas_call(
        _mla_proj_kernel,
        grid=(n // tm,),
        in_specs=[pl.BlockSpec((tm, d), lambda i: (i, 0)), full(g2), layer(w_down), full(qn2),
                  layer(w_uq), full(kvn2), layer(w_ukv),
                  pl.BlockSpec((tm, LANES), lambda i: (i, 0))],
        out_specs=[head_spec(QK_DIM), head_spec(QK_DIM), head_spec(V_HEAD_DIM + V_ONES)],
        out_shape=[jax.ShapeDtypeStruct((batch, N_HEADS, seq, QK_DIM), BF16),
                   jax.ShapeDtypeStruct((batch, N_HEADS, seq, QK_DIM), BF16),
                   jax.ShapeDtypeStruct((batch, N_HEADS, seq, V_HEAD_DIM + V_ONES), BF16)],
        scratch_shapes=[pltpu.VMEM((d, kr1 + QK_ROPE_DIM), BF16),
                        pltpu.VMEM((Q_LORA_RANK, N_HEADS * (QK_NOPE_DIM + 2 * QK_ROPE_DIM)), BF16),
                        pltpu.VMEM((KV_LORA_RANK, N_HEADS * (QK_NOPE_DIM + V_HEAD_DIM)), BF16)],
        compiler_params=_params(("arbitrary",)),
        name="mla_proj",
    )(h, g2, w_down, qn2, w_uq, kvn2, w_ukv, cs)


def _attn_kernel(q_ref, k_ref, v_ref, o_ref, m_ref, acc_ref):
    hb, t = q_ref.shape[1], q_ref.shape[2]
    qi = pl.program_id(2)

    def update(hd, r0, nr, k0, nk, causal, first=False):
        rows = slice(r0, r0 + nr)
        s = _dot_nt(q_ref[0, hd, rows, :], k_ref[0, hd, pl.ds(k0, nk), :])
        if causal:
            row = lax.broadcasted_iota(I32, s.shape, 0)
            col = lax.broadcasted_iota(I32, s.shape, 1)
            s = jnp.where(row >= col, s, -jnp.inf)
        m_cur = jnp.max(s, axis=1, keepdims=True)
        m_new = jnp.broadcast_to(m_cur, (nr, LANES)) if first else jnp.maximum(m_ref[hd, rows], m_cur)
        p = jnp.exp2(s - jnp.tile(m_new, (1, nk // LANES)))
        pv = _dot(p.astype(BF16), v_ref[0, hd, pl.ds(k0, nk), :])
        if first:
            acc_ref[hd, rows] = pv
        else:
            alpha = jnp.exp2(m_ref[hd, rows] - m_new)
            acc_ref[hd, rows] = jnp.tile(alpha, (1, acc_ref.shape[2] // LANES)) * acc_ref[hd, rows] + pv
        m_ref[hd, rows] = m_new

    d0 = pl.multiple_of(qi * t, t)
    half = t // 2
    for hd in range(hb):
        update(hd, 0, t, d0, half, True, first=True)
        update(hd, half, half, pl.multiple_of(d0 + half, half), half, True)

    def full_tile(ki, carry):
        k0 = pl.multiple_of(ki * t, t)
        for hd in range(hb):
            update(hd, 0, t, k0, t, False)
        return carry

    lax.fori_loop(0, qi, full_tile, 0)
    for hd in range(hb):
        o_ref[0, :, hd * V_HEAD_DIM:(hd + 1) * V_HEAD_DIM] = (
            acc_ref[hd, :, 0:V_HEAD_DIM] / acc_ref[hd, :, V_HEAD_DIM:V_HEAD_DIM + LANES]
        ).astype(o_ref.dtype)


def _attention(q, k, v):
    b, nh, s, _ = q.shape
    t = min(ATTN_TILE, s)
    hb = ATTN_HEADS
    kv_spec = lambda w: pl.BlockSpec((1, hb, s, w), lambda bi, gi, qi: (bi, gi, 0, 0))
    stat = pltpu.VMEM((hb, t, LANES), F32)
    return pl.pallas_call(
        _attn_kernel,
        grid=(b, nh // hb, s // t),
        in_specs=[pl.BlockSpec((1, hb, t, QK_DIM), lambda bi, gi, qi: (bi, gi, qi, 0)),
                  kv_spec(QK_DIM), kv_spec(v.shape[3])],
        out_specs=pl.BlockSpec((1, t, hb * V_HEAD_DIM), lambda bi, gi, qi: (bi, qi, gi)),
        out_shape=jax.ShapeDtypeStruct((b, s, nh * V_HEAD_DIM), BF16),
        scratch_shapes=[stat, pltpu.VMEM((hb, t, v.shape[3]), F32)],
        compiler_params=_params(("arbitrary",) * 3),
        name="mla_attention",
    )(q, k, v)


def _load_as_bf16(src_ref, dst_ref, stage_ref, sem):
    rc = stage_ref.shape[1]
    n_chunks = src_ref.shape[0] // rc

    def copy(k):
        return pltpu.make_async_copy(src_ref.at[pl.ds(k * rc, rc)], stage_ref.at[k % 2], sem.at[k % 2])

    copy(0).start()
    for k in range(n_chunks):
        if k + 1 < n_chunks:
            copy(k + 1).start()
        copy(k).wait()
        dst_ref[k * rc:(k + 1) * rc, :] = stage_ref[k % 2].astype(BF16)


def _oproj_ffn_kernel(h_ref, o_ref, g_ref, wo_hbm, wg_hbm, wu_hbm, wd_hbm, out_ref,
                      wo_ref, wg_ref, wu_ref, wd_ref, stage_d, stage_ff, sem, *, a, chunk):
    @pl.when(pl.program_id(0) == 0)
    def _():
        _load_as_bf16(wo_hbm.at[a], wo_ref, stage_d, sem)
        _load_as_bf16(wg_hbm.at[a], wg_ref, stage_ff, sem)
        _load_as_bf16(wu_hbm.at[a], wu_ref, stage_ff, sem)
        _load_as_bf16(wd_hbm.at[a], wd_ref, stage_d, sem)

    h1 = h_ref[...] + _dot(o_ref[...], wo_ref[...])
    hn = _rms(h1, g_ref[...]).astype(BF16)
    acc = h1
    ff = wg_ref.shape[1]
    for c0 in range(0, ff, chunk):
        c1 = min(c0 + chunk, ff)
        gate = _dot(hn, wg_ref[:, c0:c1])
        up = _dot(hn, wu_ref[:, c0:c1])
        act = (gate * jax.nn.sigmoid(gate) * up).astype(BF16)
        acc = acc + _dot(act, wd_ref[c0:c1, :])
    out_ref[...] = acc


def _oproj_ffn(h, o, g, wo, wg, wu, wd, a):
    n, d = h.shape
    dv, ff = wo.shape[1], wg.shape[2]
    tm = min(ROW_TILE, n)
    g2 = g[None, :]
    hbm = pl.BlockSpec(memory_space=pl.ANY)
    return pl.pallas_call(
        functools.partial(_oproj_ffn_kernel, a=a, chunk=FF_CHUNK),
        grid=(n // tm,),
        in_specs=[pl.BlockSpec((tm, d), lambda i: (i, 0)),
                  pl.BlockSpec((tm, dv), lambda i: (i, 0)),
                  pl.BlockSpec(g2.shape, lambda i: (0, 0)), hbm, hbm, hbm, hbm],
        out_specs=pl.BlockSpec((tm, d), lambda i: (i, 0)),
        out_shape=jax.ShapeDtypeStruct((n, d), F32),
        scratch_shapes=[pltpu.VMEM((dv, d), BF16), pltpu.VMEM((d, ff), BF16),
                        pltpu.VMEM((d, ff), BF16), pltpu.VMEM((ff, d), BF16),
                        pltpu.VMEM((2, WEIGHT_STAGE_ROWS, d), F32),
                        pltpu.VMEM((2, WEIGHT_STAGE_ROWS, ff), F32),
                        pltpu.SemaphoreType.DMA((2,))],
        compiler_params=_params(("arbitrary",)),
        name="oproj_ffn",
    )(h, o, g2, wo, wg, wu, wd)


def _row(ref, r):
    return ref.at[pl.ds(pl.multiple_of(r * SUBLANES, SUBLANES), SUBLANES)]


def _rows(ref, r, n_rows):
    return ref.at[pl.ds(pl.multiple_of(r * SUBLANES, SUBLANES), n_rows * SUBLANES)]


def _segment_copies(n_rows, start_copy, max_rows):
    for b in range(max_rows.bit_length()):
        size = 1 << b

        @pl.when(((n_rows >> b) & 1) == 1)
        def _():
            start_copy(n_rows & (size - 1), size)


SEG_STRIDE = 2 * N_EXPERTS


def _pool_router_kernel(h_ref, gp_ref, pw_ref, ps_ref, gf_ref, rt_ref, tri_ref,
                        out_ref, hn_ref, meta_ref, gate_ref, seg_ref, pad_ref, carry_ref):
    s, d = h_ref.shape[1], h_ref.shape[2]
    gd = d // len(POOL_WINDOWS)
    t = tri_ref.shape[0]

    @pl.when(pl.program_id(0) == 0)
    def _():
        carry_ref[...] = jnp.zeros(carry_ref.shape, F32)

    h = h_ref[0]
    hp = _rms(h, gp_ref[...])
    t1 = lax.broadcasted_iota(I32, (s, 1), 0) + 1
    pad_ref[0:POOL_HALO, :] = jnp.zeros((POOL_HALO, gd), F32)
    for gi, w in enumerate(POOL_WINDOWS):
        cols = slice(gi * gd, (gi + 1) * gd)
        x = hp[:, cols]
        run = x
        k = 1
        while k < w:
            pad_ref[POOL_HALO:POOL_HALO + s, :] = run
            run = run + pad_ref[POOL_HALO - k:POOL_HALO - k + s, :]
            k *= 2
        cnt = jnp.minimum(t1, w).astype(F32)
        z = (run / cnt - x).astype(BF16)
        y = _dot(z, pw_ref[gi].astype(BF16)) * ps_ref[:, cols]
        out_ref[0, :, cols] = h[:, cols] + y

    for j in range(s // t):
        tok = slice(j * t, (j + 1) * t)
        hn, meta, gates, seg = _route_tile(out_ref[0, tok, :], gf_ref[...], rt_ref[...],
                                           tri_ref[...], carry_ref)
        hn_ref[0, tok, :] = hn
        meta_ref[:, tok] = meta
        gate_ref[:, tok] = gates
        seg_ref[j] = seg


def _route_tile(h, g, rt, tri, carry_ref):
    t = h.shape[0]
    hn = _rms(h, g)

    hi = hn.astype(BF16)
    lo = (hn - hi.astype(F32)).astype(BF16)
    a = _dot_nt(rt, hi)
    logits = a[:N_EXPERTS] + a[N_EXPERTS:] + _dot_nt(rt[:N_EXPERTS], lo)

    eidx = lax.broadcasted_iota(I32, logits.shape, 0)
    m1 = jnp.max(logits, axis=0, keepdims=True)
    i1 = jnp.min(jnp.where(logits == m1, eidx, N_EXPERTS), axis=0, keepdims=True)
    oh1 = eidx == i1
    rest = jnp.where(oh1, -jnp.inf, logits)
    m2 = jnp.max(rest, axis=0, keepdims=True)
    i2 = jnp.min(jnp.where(rest == m2, eidx, N_EXPERTS), axis=0, keepdims=True)
    oh2 = eidx == i2
    e2 = jnp.exp(m2 - m1)
    den = 1.0 + e2
    g1 = 1.0 / den
    g2 = e2 / den

    sel = jnp.where(oh1 | oh2, 1.0, 0.0).astype(F32)
    pref = _dot(sel.astype(BF16), tri)
    carry = carry_ref[...]
    cnt = jnp.broadcast_to(jnp.sum(sel, axis=1, keepdims=True), (N_EXPERTS, LANES))
    erow = lax.broadcasted_iota(I32, cnt.shape, 0)
    incl = cnt
    for sh in (1, 2, 4):
        incl = incl + jnp.where(erow >= sh, pltpu.roll(incl, sh, axis=0), 0.0)
    lrow = (incl - cnt)[:, 0:1] + pref
    ls1 = jnp.sum(jnp.where(oh1, lrow, 0.0), axis=0, keepdims=True).astype(I32)
    ls2 = jnp.sum(jnp.where(oh2, lrow, 0.0), axis=0, keepdims=True).astype(I32)
    carry_ref[...] = carry + cnt

    zi = jnp.zeros((SUBLANES - 4, t), I32)
    meta = jnp.concatenate([i1, i2, ls1, ls2, zi], axis=0)
    zf = jnp.zeros((SUBLANES - 2, t), F32)
    gates = jnp.concatenate([g1, g2, zf], axis=0)
    return hi, meta, gates, jnp.concatenate([cnt, carry], axis=0).astype(I32)


def _pool_router(h3, g_pool, pool_w, pool_scale, g_ffn, router):
    b, s, d = h3.shape
    n = b * s
    gd = d // len(POOL_WINDOWS)
    t = min(ROUTER_TILE, s)
    r_t = router.T
    r_hi = r_t.astype(BF16)
    r_lo = (r_t - r_hi.astype(F32)).astype(BF16)
    rt = jnp.concatenate([r_hi, r_lo], axis=0)
    tri = jnp.triu(jnp.ones((t, t), BF16), k=1)
    full = lambda a: pl.BlockSpec(a.shape, lambda i: (0,) * a.ndim)
    gp2, ps2, gf2 = g_pool[None, :], pool_scale[None, :], g_ffn[None, :]
    return pl.pallas_call(
        _pool_router_kernel,
        grid=(b,),
        in_specs=[pl.BlockSpec((1, s, d), lambda i: (i, 0, 0)), full(gp2), full(pool_w), full(ps2),
                  full(gf2), full(rt), full(tri)],
        out_specs=[pl.BlockSpec((1, s, d), lambda i: (i, 0, 0)),
                   pl.BlockSpec((1, s, d), lambda i: (i, 0, 0)),
                   pl.BlockSpec((SUBLANES, s), lambda i: (0, i)),
                   pl.BlockSpec((SUBLANES, s), lambda i: (0, i)),
                   pl.BlockSpec((s // t, SEG_STRIDE, LANES), lambda i: (i, 0, 0))],
        out_shape=[jax.ShapeDtypeStruct((b, s, d), F32),
                   jax.ShapeDtypeStruct((b, s, d), BF16),
                   jax.ShapeDtypeStruct((SUBLANES, n), I32),
                   jax.ShapeDtypeStruct((SUBLANES, n), F32),
                   jax.ShapeDtypeStruct((n // t, SEG_STRIDE, LANES), I32)],
        scratch_shapes=[pltpu.VMEM((s + POOL_HALO, gd), F32), pltpu.VMEM((N_EXPERTS, LANES), F32)],
        compiler_params=_params(("arbitrary",)),
        name="pool_router",
    )(h3, gp2, pool_w, ps2, gf2, rt, tri)


def _dispatch_kernel(seg_ref, zb_ref, hn_ref, mrow_ref, xb_ref, buf_ref, zero_ref, sem, zsem,
                     *, n_zero):
    t = hn_ref.shape[0]
    i = pl.program_id(0)
    steps = pl.num_programs(0)
    slot = lax.rem(i, 2)

    def wait_rows(n_rows, s):
        span = _rows(xb_ref, 0, n_rows)
        pltpu.make_async_copy(span, span, s).wait()

    @pl.when(i == 0)
    def _():
        zero_ref[...] = jnp.zeros(zero_ref.shape, F32)
        for e in range(N_EXPERTS):
            def zstart(r, c):
                pltpu.make_async_copy(zero_ref, _row(xb_ref, r), zsem).start()
                return c
            lax.fori_loop(zb_ref[e], zb_ref[N_EXPERTS + e], zstart, 0)

    srow = lax.broadcasted_iota(I32, (2 * t, t), 0)
    perm = jnp.where((srow == mrow_ref[2:3, :]) | (srow == mrow_ref[3:4, :]), 1.0, 0.0).astype(BF16)
    xs = _dot(perm, hn_ref[...])

    @pl.when(i >= 2)
    def _():
        wait_rows(2 * t, sem.at[slot])
    buf = buf_ref.at[slot]
    _rows_to_tiles(buf, xs)

    l_e = 0
    for e in range(N_EXPERTS):
        c_e = seg_ref[i * SEG_STRIDE + e]

        def start_copy(off, size, src=l_e, dst=seg_ref[i * SEG_STRIDE + N_EXPERTS + e]):
            pltpu.make_async_copy(_rows(buf, src + off, size), _rows(xb_ref, dst + off, size),
                                  sem.at[slot]).start()
        _segment_copies(c_e, start_copy, t)
        l_e = l_e + c_e

    @pl.when(i == steps - 1)
    def _():
        wait_rows(2 * t, sem.at[slot])

        @pl.when(steps > 1)
        def _():
            wait_rows(2 * t, sem.at[1 - slot])
        wait_rows(n_zero, zsem)


def _dispatch(hn, meta, seg, zero_bounds, n_rows):
    n, d = hn.shape
    t = min(ROUTER_TILE, n)
    return pl.pallas_call(
        functools.partial(_dispatch_kernel, n_zero=n_rows - 2 * n),
        grid_spec=pltpu.PrefetchScalarGridSpec(
            num_scalar_prefetch=2,
            grid=(n // t,),
            in_specs=[pl.BlockSpec((t, d), lambda i, sg, zb: (i, 0)),
                      pl.BlockSpec((SUBLANES, t), lambda i, sg, zb: (0, i))],
            out_specs=pl.BlockSpec(memory_space=pl.ANY),
            scratch_shapes=[pltpu.VMEM((2, 2 * t * SUBLANES, LANES), F32),
                            pltpu.VMEM((SUBLANES, LANES), F32),
                            pltpu.SemaphoreType.DMA((2,)),
                            pltpu.SemaphoreType.DMA],
        ),
        out_shape=jax.ShapeDtypeStruct((n_rows * SUBLANES, LANES), F32),
        compiler_params=_params(("arbitrary",)),
        name="moe_dispatch",
    )(seg, zero_bounds, hn, meta)


def _expert_kernel(be_ref, x_ref, wgt_ref, wut_ref, wd_ref, y_ref, wgu_s, wd_s):
    i = pl.program_id(0)
    ff = wgt_ref.shape[1]

    @pl.when((i == 0) | (be_ref[i] != be_ref[jnp.maximum(i - 1, 0)]))
    def _():
        for j in range(wgu_s.shape[1] // (2 * LANES)):
            c0 = j * LANES
            w = min(LANES, ff - c0)
            r0 = c0 + w - LANES
            for half, ref in ((0, wgt_ref), (1, wut_ref)):
                s0 = (2 * j + half) * LANES
                cols = ref[0, r0:r0 + LANES, :].T
                wgu_s[:, s0:s0 + w] = cols[:, LANES - w:].astype(BF16)
                if w < LANES:
                    wgu_s[:, s0 + w:s0 + LANES] = jnp.zeros((wgu_s.shape[0], LANES - w), BF16)
        wd_s[0:ff, :] = wd_ref[0].astype(BF16)
        if wd_s.shape[0] > ff:
            wd_s[ff:, :] = jnp.zeros((wd_s.shape[0] - ff, wd_s.shape[1]), BF16)

    bm = x_ref.shape[0] // SUBLANES
    x = _tiles_to_rows(x_ref, bm).astype(BF16)
    gu = _dot(x, wgu_s[...])
    acts = []
    for c0 in range(0, gu.shape[1], 2 * LANES):
        gate = gu[:, c0:c0 + LANES]
        up = gu[:, c0 + LANES:c0 + 2 * LANES]
        acts.append((gate * jax.nn.sigmoid(gate) * up).astype(BF16))
    act = jnp.concatenate(acts, axis=-1)
    _rows_to_tiles(y_ref, _dot(act, wd_s[...]))


def _experts(xb, block_e, wg, wu, wd, a, bm):
    d, ff = wg.shape[2], wg.shape[3]
    ffp = -(-ff // LANES) * LANES
    w_spec = lambda r, c: pl.BlockSpec((None, 1, r, c), lambda i, be: (a, be[i], 0, 0))
    wg = jnp.swapaxes(wg, 2, 3)
    wu = jnp.swapaxes(wu, 2, 3)
    return pl.pallas_call(
        _expert_kernel,
        grid_spec=pltpu.PrefetchScalarGridSpec(
            num_scalar_prefetch=1,
            grid=(block_e.shape[0],),
            in_specs=[pl.BlockSpec((bm * SUBLANES, LANES), lambda i, be: (i, 0)),
                      w_spec(ff, d), w_spec(ff, d), w_spec(ff, d)],
            out_specs=pl.BlockSpec((bm * SUBLANES, LANES), lambda i, be: (i, 0)),
            scratch_shapes=[pltpu.VMEM((d, 2 * ffp), BF16), pltpu.VMEM((ffp, d), BF16)],
        ),
        out_shape=jax.ShapeDtypeStruct(xb.shape, F32),
        compiler_params=_params(("arbitrary",)),
        name="moe_experts",
    )(block_e, xb, wg, wu, wd)


def _combine_kernel(seg_ref, h_ref, mrow_ref, grow_ref, mcol_ref, fn_ref, yb_ref, out_ref,
                    buf_ref, sem, *, final_norm):
    t = h_ref.shape[0]
    i = pl.program_id(0)
    steps = pl.num_programs(0)
    slot = lax.rem(i, 2)

    def fetch(tile, sl):
        buf = buf_ref.at[sl]
        l_e = 0
        for e in range(N_EXPERTS):
            c_e = seg_ref[tile * SEG_STRIDE + e]

            def start_copy(off, size, src=seg_ref[tile * SEG_STRIDE + N_EXPERTS + e], dst=l_e):
                pltpu.make_async_copy(_rows(yb_ref, src + off, size), _rows(buf, dst + off, size),
                                      sem.at[sl]).start()
            _segment_copies(c_e, start_copy, t)
            l_e = l_e + c_e

    @pl.when(i == 0)
    def _():
        fetch(0, 0)

    @pl.when(i + 1 < steps)
    def _():
        fetch(i + 1, 1 - slot)

    buf = buf_ref.at[slot]
    pltpu.make_async_copy(_rows(yb_ref, 0, 2 * t), buf, sem.at[slot]).wait()
    ys = _tiles_to_rows(buf, 2 * t)

    ls1 = mrow_ref[2:3, :]
    ls2 = mrow_ref[3:4, :]
    srow = lax.broadcasted_iota(I32, (2 * t, t), 0)
    w = (jnp.where(srow == ls1, grow_ref[0:1, :], 0.0)
         + jnp.where(srow == ls2, grow_ref[1:2, :], 0.0))
    ysg = ys * jnp.sum(w, axis=1, keepdims=True)
    hi = ysg.astype(BF16)
    lo = (ysg - hi.astype(F32)).astype(BF16)
    scol = lax.broadcasted_iota(I32, (t, 2 * t), 1)
    unsort = jnp.where((scol == mcol_ref[:, 0:1]) | (scol == mcol_ref[:, 1:2]), 1.0, 0.0).astype(BF16)
    out = h_ref[...] + _dot(unsort, hi) + _dot(unsort, lo)
    if final_norm:
        out = _rms(out, fn_ref[...])
    out_ref[...] = out


def _combine(h, yb, seg, meta, gates, fnorm, final_norm):
    n, d = h.shape
    t = min(ROUTER_TILE, n)
    mcol = meta[2:4].T
    fn2 = fnorm[None, :]
    return pl.pallas_call(
        functools.partial(_combine_kernel, final_norm=final_norm),
        grid_spec=pltpu.PrefetchScalarGridSpec(
            num_scalar_prefetch=1,
            grid=(n // t,),
            in_specs=[pl.BlockSpec((t, d), lambda i, sg: (i, 0)),
                      pl.BlockSpec((SUBLANES, t), lambda i, sg: (0, i)),
                      pl.BlockSpec((SUBLANES, t), lambda i, sg: (0, i)),
                      pl.BlockSpec((t, 2), lambda i, sg: (i, 0)),
                      pl.BlockSpec(fn2.shape, lambda i, sg: (0, 0)),
                      pl.BlockSpec(memory_space=pl.ANY)],
            out_specs=pl.BlockSpec((t, d), lambda i, sg: (i, 0)),
            scratch_shapes=[pltpu.VMEM((2, 2 * t * SUBLANES, LANES), F32),
                            pltpu.SemaphoreType.DMA((2,))],
        ),
        out_shape=jax.ShapeDtypeStruct((n, d), F32),
        compiler_params=_params(("arbitrary",)),
        name="moe_combine",
    )(seg, h, meta, gates, mcol, fn2, yb)


def _moe(h, hn, meta, gates, segv, wg, wu, wd, a, fnorm, final_norm):
    n = h.shape[0]
    bm = min(MOE_ROWS, n)
    tile_cnt = segv[:, :N_EXPERTS, 0]
    tile_base = segv[:, N_EXPERTS:, 0]
    counts = tile_cnt[-1] + tile_base[-1]
    padded = (counts + bm - 1) // bm * bm
    pad_end = jnp.cumsum(padded)
    pad_start = pad_end - padded
    n_blocks = (n * 2) // bm + N_EXPERTS
    n_rows = n_blocks * bm
    seg = jnp.concatenate([tile_cnt, tile_base + pad_start[None, :]], axis=1).reshape(-1).astype(I32)
    zero_bounds = jnp.concatenate([pad_start + counts, pad_start[1:],
                                   jnp.array([n_rows], I32)]).astype(I32)
    first_row = jnp.arange(n_blocks, dtype=I32)[:, None] * bm
    block_e = jnp.minimum(jnp.sum((pad_end[None, :] <= first_row).astype(I32), axis=1),
                          N_EXPERTS - 1).astype(I32)
    xb = _dispatch(hn, meta, seg, zero_bounds, n_rows)
    yb = _experts(xb, block_e, wg, wu, wd, a, bm)
    return _combine(h, yb, seg, meta, gates, fnorm, final_norm)


def kernel(x, positions, attn_norm, ffn_norm, mla_w_down, mla_q_norm, mla_w_uq, mla_kv_norm,
           mla_w_ukv, mla_w_o, pool_w, pool_scale, ffn_w_gate, ffn_w_up, ffn_w_down,
           moe_router, moe_w_gate, moe_w_up, moe_w_down, final_norm):
    b, s, d = x.shape
    n = b * s
    depth = attn_norm.shape[0]
    assert depth % 2 == 0, "layers come in (attention + dense FFN, pooling + MoE) pairs"
    h = x.reshape(n, d)
    cs = _rope_table(positions)
    for layer in range(0, depth, 2):
        a = layer // 2
        q, k, v = _mla_proj(h, attn_norm[layer], mla_w_down, mla_q_norm[a], mla_w_uq,
                            mla_kv_norm[a], mla_w_ukv, a, cs, b, s)
        o = _attention(q, k, v).reshape(n, N_HEADS * V_HEAD_DIM)
        h = _oproj_ffn(h, o, ffn_norm[layer], mla_w_o, ffn_w_gate, ffn_w_up, ffn_w_down, a)
        h3, hn3, meta, gates, segv = _pool_router(h.reshape(b, s, d), attn_norm[layer + 1],
                                                  pool_w[a], pool_scale[a], ffn_norm[layer + 1],
                                                  moe_router[a])
        h = _moe(h3.reshape(n, d), hn3.reshape(n, d), meta, gates, segv, moe_w_gate, moe_w_up,
                 moe_w_down, a, final_norm, layer + 2 == depth)
    return h.reshape(b, s, d)
```
